```python
import jax, jax.numpy as jnp
from jax import lax
import numpy as np

D_MODEL = 2048
BATCH = 4
SEQ = 2048
DEPTH = 1
DEC_BATCH = 128
DEC_SEQ = 4
PAST_LEN = 16384
PAGE_SIZE = 128

D_CONF = D_MODEL
CONF_WIDTH = 31
D_SC = D_MODEL
SC_WIDTH = 3
N_PEER_HEADS = 8
N_KEYS = 128
N_EXPERTS = N_KEYS * N_KEYS
TOPK = 16
D_QUERY = D_MODEL // N_PEER_HEADS
D_SUBKEY = D_QUERY // 2
PEER_BLOCK = 128
ALPHA = (2.0 * DEPTH) ** 0.25
BETA = (8.0 * DEPTH) ** -0.25
LN_EPS = 1e-5
N_IN_COLS = 2 * D_CONF + 3 * D_SC + 2 * D_MODEL
IN_SPLITS = (D_CONF, 2 * D_CONF, 2 * D_CONF + D_SC, 2 * D_CONF + 2 * D_SC,
             2 * D_CONF + 3 * D_SC, 2 * D_CONF + 3 * D_SC + D_MODEL)

kernel_name = 'conformer_shortconv_peer_hybrid_step'


def _layernorm(x, g, b):
    xf = x.astype(jnp.float32)
    mu = jnp.mean(xf, -1, keepdims=True)
    var = jnp.mean(jnp.square(xf - mu), -1, keepdims=True)
    y = (xf - mu) * lax.rsqrt(var + LN_EPS) * g.astype(jnp.float32) + b.astype(jnp.float32)
    return y.astype(x.dtype)


def _causal_dwconv(buf, x, w):
    xp = jnp.concatenate([buf, x], axis=1)
    y = lax.conv_general_dilated(xp, w[:, None, :], window_strides=(1,), padding='VALID',
                                 dimension_numbers=('NWC', 'WIO', 'NWC'),
                                 feature_group_count=x.shape[-1])
    return y, xp[:, -(w.shape[0] - 1):]


def _peer(u, w_q, sub_keys, peer_u, peer_v):
    T = u.shape[0]
    n_blk = -(-T // PEER_BLOCK)
    up = jnp.pad(u, ((0, n_blk * PEER_BLOCK - T), (0, 0))).reshape(n_blk, PEER_BLOCK, D_MODEL)

    def block(ub):
        q = (ub @ w_q).reshape(PEER_BLOCK, N_PEER_HEADS, 2, D_SUBKEY)
        s = jnp.einsum('thpk,hpnk->thpn', q, sub_keys).astype(jnp.float32)
        v, i = lax.top_k(s, TOPK)
        cand = (v[:, :, 0, :, None] + v[:, :, 1, None, :]).reshape(PEER_BLOCK, N_PEER_HEADS, TOPK * TOPK)
        sc, pos = lax.top_k(cand, TOPK)
        i1 = jnp.take_along_axis(i[:, :, 0], pos // TOPK, axis=-1)
        i2 = jnp.take_along_axis(i[:, :, 1], pos % TOPK, axis=-1)
        eid = (i1 * N_KEYS + i2).reshape(PEER_BLOCK, N_PEER_HEADS * TOPK)
        g = jax.nn.softmax(sc, axis=-1).reshape(PEER_BLOCK, N_PEER_HEADS * TOPK).astype(ub.dtype)
        act = jax.nn.gelu(jnp.einsum('td,tkd->tk', ub, peer_u[eid]), approximate=False)
        return jnp.einsum('tk,tkd->td', g * act, peer_v[eid])

    out = lax.map(block, up)
    return out.reshape(n_blk * PEER_BLOCK, D_MODEL)[:T]


def _layer(x, c, buf_a, buf_b, w_mod, b_mod, w_in, conv_a_w, conv_a_b, ln_a_g, ln_a_b, w_a_out,
           conv_b_w, w_b_out, w_o, ln1_g, ln1_b, w_q, sub_keys, peer_u, peer_v, ln2_g, ln2_b):
    N, L, _ = x.shape
    mod = jax.nn.silu(c) @ w_mod + b_mod
    sh1, sc1, g1, sh2, sc2, g2 = jnp.split(mod[:, None, :], 6, axis=-1)

    u = x * (1 + sc1) + sh1
    z = u @ w_in
    a_val, a_gate, b_gate, c_gate, h, gate_a, gate_b = jnp.split(z, IN_SPLITS, axis=-1)
    glu = a_val * jax.nn.sigmoid(a_gate)
    ya, new_a = _causal_dwconv(buf_a, glu, conv_a_w)
    ya = jax.nn.silu(_layernorm(ya + conv_a_b, ln_a_g, ln_a_b)) @ w_a_out
    yb, new_b = _causal_dwconv(buf_b, c_gate * h, conv_b_w)
    yb = (b_gate * yb) @ w_b_out
    m = (jax.nn.sigmoid(gate_a) * ya + jax.nn.sigmoid(gate_b) * yb) @ w_o
    x = _layernorm(ALPHA * x + (1 + g1) * m, ln1_g, ln1_b)

    u2 = x * (1 + sc2) + sh2
    f = _peer(u2.reshape(N * L, D_MODEL), w_q, sub_keys, peer_u, peer_v).reshape(N, L, D_MODEL)
    x = _layernorm(ALPHA * x + (1 + g2) * f, ln2_g, ln2_b)
    return x, new_a, new_b


def setup_inputs(seed: int = 0) -> dict:
    key = jax.random.key(seed)
    ks = jax.random.split(key, 25)
    D = D_MODEL
    L = DEPTH

    def nrm(k, shape, s):
        return jax.random.normal(k, shape, jnp.float32) * s

    return {
        'x_prompt': nrm(ks[0], (BATCH, SEQ, D), 1.0),
        'x_sample': nrm(ks[1], (DEC_BATCH, DEC_SEQ, D), 1.0),
        'state_conv_a': nrm(ks[2], (L, DEC_BATCH, CONF_WIDTH - 1, D_CONF), 0.5),
        'state_conv_b': nrm(ks[3], (L, DEC_BATCH, SC_WIDTH - 1, D_SC), 0.5),
        'c_prompt': nrm(ks[4], (BATCH, D), 1.0),
        'c_sample': nrm(ks[5], (DEC_BATCH, D), 1.0),
        'w_mod': nrm(ks[6], (L, D, 6 * D), 0.1 * D ** -0.5),
        'b_mod': nrm(ks[7], (L, 6 * D), 0.01),
        'w_in': nrm(ks[8], (L, D, N_IN_COLS), D ** -0.5),
        'conv_a_w': nrm(ks[9], (L, CONF_WIDTH, D_CONF), CONF_WIDTH ** -0.5),
        'conv_a_b': nrm(ks[10], (L, D_CONF), 0.01),
        'ln_a_g': 1.0 + nrm(ks[11], (L, D_CONF), 0.02),
        'ln_a_b': nrm(ks[12], (L, D_CONF), 0.02),
        'w_a_out': nrm(ks[13], (L, D_CONF, D), D_CONF ** -0.5),
        'conv_b_w': nrm(ks[14], (L, SC_WIDTH, D_SC), SC_WIDTH ** -0.5),
        'w_b_out': nrm(ks[15], (L, D_SC, D), D_SC ** -0.5),
        'w_o': nrm(ks[16], (L, D, D), BETA * D ** -0.5),
        'ln1_g': 1.0 + nrm(ks[17], (L, D), 0.02),
        'ln1_b': nrm(ks[18], (L, D), 0.02),
        'w_q': nrm(ks[19], (L, D, N_PEER_HEADS * D_QUERY), D ** -0.5),
        'sub_keys': nrm(ks[20], (L, N_PEER_HEADS, 2, N_KEYS, D_SUBKEY), D_SUBKEY ** -0.5),
        'peer_u': nrm(ks[21], (L, N_EXPERTS, D), D ** -0.5),
        'peer_v': nrm(ks[22], (L, N_EXPERTS, D), BETA),
        'ln2_g': 1.0 + nrm(ks[23], (L, D), 0.02),
        'ln2_b': nrm(ks[24], (L, D), 0.02),
    }


def reference(x_prompt, x_sample, state_conv_a, state_conv_b, c_prompt, c_sample,
              w_mod, b_mod, w_in, conv_a_w, conv_a_b, ln_a_g, ln_a_b, w_a_out,
              conv_b_w, w_b_out, w_o, ln1_g, ln1_b, w_q, sub_keys, peer_u, peer_v,
              ln2_g, ln2_b):
    layer_w = (w_mod, b_mod, w_in, conv_a_w, conv_a_b, ln_a_g, ln_a_b, w_a_out,
               conv_b_w, w_b_out, w_o, ln1_g, ln1_b, w_q, sub_keys, peer_u, peer_v,
               ln2_g, ln2_b)
    nb = x_prompt.shape[0]
    zero_a = jnp.zeros((nb, CONF_WIDTH - 1, D_CONF), x_prompt.dtype)
    zero_b = jnp.zeros((nb, SC_WIDTH - 1, D_SC), x_prompt.dtype)
    yp, ys = x_prompt, x_sample
    na_p, nb_p, na_s, nb_s = [], [], [], []
    for l in range(DEPTH):
        wl = [w[l] for w in layer_w]
        yp, a_p, b_p = _layer(yp, c_prompt, zero_a, zero_b, *wl)
        ys, a_s, b_s = _layer(ys, c_sample, state_conv_a[l], state_conv_b[l], *wl)
        na_p.append(a_p)
        nb_p.append(b_p)
        na_s.append(a_s)
        nb_s.append(b_s)
    new_conv_a_prompt = jnp.stack(na_p)
    new_conv_b_prompt = jnp.stack(nb_p)
    new_conv_a_sample = jnp.stack(na_s)
    new_conv_b_sample = jnp.stack(nb_s)
    return (yp, ys, new_conv_a_prompt, new_conv_b_prompt, new_conv_a_sample, new_conv_b_sample)
```

```python
import functools

import jax
import jax.numpy as jnp
from jax import lax
from jax.experimental import pallas as pl
from jax.experimental.pallas import tpu as pltpu

F32 = jnp.float32
BF16 = jnp.bfloat16

D = 2048
CONF_W = 31
SC_W = 3
N_HEADS = 8
N_KEYS = 128
TOPK = 16
LN_EPS = 1e-5
INV_SQRT2 = 0.7071067811865476
NEG_BIG = -3.0e38

LANES = 128
VMEM_LIMIT = 56 * 1024 * 1024

_CAND = tuple((r, c) for r in range(TOPK) for c in range(TOPK) if (r + 1) * (c + 1) <= TOPK)


def _params(*sem):
    return pltpu.CompilerParams(dimension_semantics=sem, vmem_limit_bytes=VMEM_LIMIT)


def _layernorm(x, g, b):
    mu = jnp.mean(x, axis=-1, keepdims=True)
    xc = x - mu
    var = jnp.mean(xc * xc, axis=-1, keepdims=True)
    return xc * lax.rsqrt(var + LN_EPS) * g + b


def _mod_spec_fn(per_token, tt, tiles_per_seq):
    def spec(k):
        if per_token:
            return pl.BlockSpec((None, tt, D), lambda i, *_: (k, i, 0))
        return pl.BlockSpec((None, None, 1, D), lambda i, *_: (i // tiles_per_seq, k, 0, 0))
    return spec


def _mod_kernel(c_ref, w_ref, b_ref, o_ref):
    c = c_ref[...]
    s = (c * jax.nn.sigmoid(c)).astype(BF16)
    o_ref[...] = jnp.dot(s, w_ref[...].astype(BF16), preferred_element_type=F32) + b_ref[...]


def _modulation(c_all, w_mod, b_mod):
    n = c_all.shape[0]
    tn = 1024
    return pl.pallas_call(
        _mod_kernel,
        grid=(6 * D // tn,),
        in_specs=[pl.BlockSpec((n, D), lambda j: (0, 0)),
                  pl.BlockSpec((D, tn), lambda j: (0, j)),
                  pl.BlockSpec((1, tn), lambda j: (0, j))],
        out_specs=pl.BlockSpec((n, tn), lambda j: (0, j)),
        out_shape=jax.ShapeDtypeStruct((n, 6 * D), F32),
        compiler_params=_params("arbitrary"),
        name="modulation",
    )(c_all, w_mod, b_mod.reshape(1, 6 * D))


def _inproj_kernel(x_ref, sc_ref, sh_ref, w0, w1, w2, w3, w4, w5, w6,
                   glu_ref, cgh_ref, bg_ref, sga_ref, sgb_ref):
    u = (x_ref[...] * (1.0 + sc_ref[...]) + sh_ref[...]).astype(BF16)

    def z(w):
        return jnp.dot(u, w[...], preferred_element_type=F32)

    glu_ref[...] = z(w0) * jax.nn.sigmoid(z(w1))
    bg_ref[...] = z(w2)
    cgh_ref[...] = z(w3) * z(w4)
    sga_ref[...] = jax.nn.sigmoid(z(w5))
    sgb_ref[...] = jax.nn.sigmoid(z(w6))


def _in_proj(x2d, mod, per_token, tiles_per_seq, w_in, tt=512, tn=256):
    T = x2d.shape[0]
    spec = _mod_spec_fn(per_token, tt, tiles_per_seq)
    nj = D // tn
    w_specs = [pl.BlockSpec((D, tn), lambda i, j, g=g: (0, g * nj + j)) for g in range(7)]
    o_spec = pl.BlockSpec((tt, tn), lambda i, j: (i, j))
    return pl.pallas_call(
        _inproj_kernel,
        grid=(T // tt, nj),
        in_specs=[pl.BlockSpec((tt, D), lambda i, j: (i, 0)), spec(1), spec(0)] + w_specs,
        out_specs=[o_spec] * 5,
        out_shape=[jax.ShapeDtypeStruct((T, D), F32)] * 5,
        compiler_params=_params("arbitrary", "arbitrary"),
        name="in_proj",
    )(x2d, mod, mod, *([w_in] * 7))


CONV_CC = 256
CONV_RC = 64
HALO_A = 32
HALO_B = 8


def _conv_prompt_kernel(g_ref, gh_ref, c_ref, ch_ref, bg_ref, wa_ref, ba_ref, lg_ref, lb_ref,
                        wb_ref, ya_ref, yb_ref, ext_ref, extb_ref, pre_ref, *, tt, tiles_per_seq):
    i = pl.program_id(0)
    first = i % tiles_per_seq == 0

    @pl.when(first)
    def _():
        ext_ref[0:HALO_A, :] = jnp.zeros((HALO_A, D), F32)
        extb_ref[0:HALO_B, :] = jnp.zeros((HALO_B, D), F32)

    @pl.when(jnp.logical_not(first))
    def _():
        ext_ref[0:HALO_A, :] = gh_ref[...]
        extb_ref[0:HALO_B, :] = ch_ref[...]

    ext_ref[HALO_A:, :] = g_ref[...]
    extb_ref[HALO_B:, :] = c_ref[...]

    def chunk(c, carry):
        cs = pl.ds(pl.multiple_of(c * CONV_CC, CONV_CC), CONV_CC)
        wa = wa_ref[:, cs]
        wb = wb_ref[:, cs]
        bias = jnp.broadcast_to(ba_ref[:, cs], (CONV_RC, CONV_CC))
        for r in range(tt // CONV_RC):
            r0 = r * CONV_RC
            acc = bias
            for k in range(CONF_W):
                acc = acc + ext_ref[pl.ds(r0 + HALO_A - (CONF_W - 1) + k, CONV_RC), cs] * wa[k:k + 1, :]
            pre_ref[pl.ds(r0, CONV_RC), cs] = acc
            accb = extb_ref[pl.ds(r0 + HALO_B - (SC_W - 1), CONV_RC), cs] * wb[0:1, :]
            for k in range(1, SC_W):
                accb = accb + extb_ref[pl.ds(r0 + HALO_B - (SC_W - 1) + k, CONV_RC), cs] * wb[k:k + 1, :]
            yb_ref[pl.ds(r0, CONV_RC), cs] = (bg_ref[pl.ds(r0, CONV_RC), cs] * accb).astype(BF16)
        return carry

    lax.fori_loop(0, D // CONV_CC, chunk, 0)
    y = _layernorm(pre_ref[...], lg_ref[...], lb_ref[...])
    ya_ref[...] = (y * jax.nn.sigmoid(y)).astype(BF16)


def _conv_prompt(glu, cgh, bg, wa, ba, lg, lb, wb, seq_len, tt=256):
    T = glu.shape[0]
    tiles_per_seq = seq_len // tt
    row = pl.BlockSpec((tt, D), lambda i: (i, 0))
    const = lambda shape: pl.BlockSpec(shape, lambda i: (0, 0))
    halo_a = pl.BlockSpec((HALO_A, D), lambda i: (jnp.maximum(i * (tt // HALO_A) - 1, 0), 0))
    halo_b = pl.BlockSpec((HALO_B, D), lambda i: (jnp.maximum(i * (tt // HALO_B) - 1, 0), 0))
    return pl.pallas_call(
        functools.partial(_conv_prompt_kernel, tt=tt, tiles_per_seq=tiles_per_seq),
        grid=(T // tt,),
        in_specs=[row, halo_a, row, halo_b, row, const((32, D)), const((1, D)), const((1, D)),
                  const((1, D)), const((8, D))],
        out_specs=[row, row],
        out_shape=[jax.ShapeDtypeStruct((T, D), BF16)] * 2,
        scratch_shapes=[pltpu.VMEM((tt + HALO_A, D), F32), pltpu.VMEM((tt + HALO_B, D), F32),
                        pltpu.VMEM((tt, D), F32)],
        compiler_params=_params("arbitrary"),
        name="conv_prompt",
    )(glu, glu, cgh, cgh, bg, wa, ba, lg, lb, wb)


def _conv_sample_kernel(g_ref, sa_ref, c_ref, sb_ref, bg_ref, wa_ref, ba_ref, lg_ref, lb_ref,
                        wb_ref, ya_ref, yb_ref, pre_ref, *, n_pos, bs):
    n_hist_a = CONF_W - 1
    n_hist_b = SC_W - 1

    def chunk(c, carry):
        cs = pl.ds(pl.multiple_of(c * CONV_CC, CONV_CC), CONV_CC)
        wa = wa_ref[:, cs]
        wb = wb_ref[:, cs]
        bias = jnp.broadcast_to(ba_ref[:, cs], (bs, CONV_CC))
        acc = [bias] * n_pos
        for j in range(n_hist_a + n_pos):
            row = sa_ref[:, j, cs] if j < n_hist_a else g_ref[j - n_hist_a, :, cs]
            for p in range(n_pos):
                k = j - p
                if 0 <= k < CONF_W:
                    acc[p] = acc[p] + row * wa[k:k + 1, :]
        for p in range(n_pos):
            pre_ref[p, :, cs] = acc[p]
        rows_b = [sb_ref[:, j, cs] for j in range(n_hist_b)] + [c_ref[p, :, cs] for p in range(n_pos)]
        for p in range(n_pos):
            accb = rows_b[p] * wb[0:1, :]
            for k in range(1, SC_W):
                accb = accb + rows_b[p + k] * wb[k:k + 1, :]
            yb_ref[p, :, cs] = (bg_ref[p, :, cs] * accb).astype(BF16)
        return carry

    lax.fori_loop(0, D // CONV_CC, chunk, 0)
    for p in range(n_pos):
        y = _layernorm(pre_ref[p], lg_ref[...], lb_ref[...])
        ya_ref[p] = (y * jax.nn.sigmoid(y)).astype(BF16)


def _conv_sample(glu, cgh, bg, state_a, state_b, wa, ba, lg, lb, wb, bs=32):
    n_pos, n_seq, _ = glu.shape
    slab = pl.BlockSpec((n_pos, bs, D), lambda i: (0, i, 0))
    const = lambda shape: pl.BlockSpec(shape, lambda i: (0, 0))
    return pl.pallas_call(
        functools.partial(_conv_sample_kernel, n_pos=n_pos, bs=bs),
        grid=(n_seq // bs,),
        in_specs=[slab, pl.BlockSpec((bs, CONF_W - 1, D), lambda i: (i, 0, 0)),
                  slab, pl.BlockSpec((bs, SC_W - 1, D), lambda i: (i, 0, 0)),
                  slab, const((32, D)), const((1, D)), const((1, D)), const((1, D)), const((8, D))],
        out_specs=[slab, slab],
        out_shape=[jax.ShapeDtypeStruct((n_pos, n_seq, D), BF16)] * 2,
        scratch_shapes=[pltpu.VMEM((n_pos, bs, D), F32)],
        compiler_params=_params("arbitrary"),
        name="conv_sample",
    )(glu, state_a, cgh, state_b, bg, wa, ba, lg, lb, wb)


def _outproj_kernel(ya_ref, yb_ref, sga_ref, sgb_ref, x_ref, g1_ref, sc2_ref, sh2_ref,
                    wa_ref, wb_ref, wo_ref, lg_ref, lb_ref, x1_ref, u2_ref, acc_ref, *, alpha, nj):
    j = pl.program_id(1)
    ya = jnp.dot(ya_ref[...], wa_ref[...], preferred_element_type=F32)
    yb = jnp.dot(yb_ref[...], wb_ref[...], preferred_element_type=F32)
    mm = (sga_ref[...] * ya + sgb_ref[...] * yb).astype(BF16)
    contrib = jnp.dot(mm, wo_ref[...], preferred_element_type=F32)

    @pl.when(j == 0)
    def _():
        acc_ref[...] = contrib

    @pl.when(j > 0)
    def _():
        acc_ref[...] += contrib

    @pl.when(j == nj - 1)
    def _():
        h = alpha * x_ref[...] + (1.0 + g1_ref[...]) * acc_ref[...]
        x1 = _layernorm(h, lg_ref[...], lb_ref[...])
        x1_ref[...] = x1
        u2_ref[...] = (x1 * (1.0 + sc2_ref[...]) + sh2_ref[...]).astype(BF16)


def _out_proj(ya_in, yb_in, sga, sgb, x2d, mod, per_token, tiles_per_seq, w_a, w_b, w_o,
              lg, lb, alpha, tt=512, tn=256):
    T = x2d.shape[0]
    spec = _mod_spec_fn(per_token, tt, tiles_per_seq)
    nj = D // tn
    row = pl.BlockSpec((tt, D), lambda i, j: (i, 0))
    col = pl.BlockSpec((tt, tn), lambda i, j: (i, j))
    const = pl.BlockSpec((1, D), lambda i, j: (0, 0))
    return pl.pallas_call(
        functools.partial(_outproj_kernel, alpha=alpha, nj=nj),
        grid=(T // tt, nj),
        in_specs=[row, row, col, col, row, spec(2), spec(4), spec(3),
                  pl.BlockSpec((D, tn), lambda i, j: (0, j)),
                  pl.BlockSpec((D, tn), lambda i, j: (0, j)),
                  pl.BlockSpec((tn, D), lambda i, j: (j, 0)),
                  const, const],
        out_specs=[row, row],
        out_shape=[jax.ShapeDtypeStruct((T, D), F32), jax.ShapeDtypeStruct((T, D), BF16)],
        scratch_shapes=[pltpu.VMEM((tt, D), F32)],
        compiler_params=_params("arbitrary", "arbitrary"),
        name="out_proj",
    )(ya_in, yb_in, sga, sgb, x2d, mod, mod, mod, w_a, w_b, w_o, lg, lb)


def _score_kernel(u2_ref, wq_ref, keys_ref, st_ref):
    q = jnp.dot(u2_ref[...], wq_ref[...], preferred_element_type=F32).astype(BF16)
    for c in range(2 * N_HEADS):
        st_ref[c * N_KEYS:(c + 1) * N_KEYS, :] = lax.dot_general(
            keys_ref[c], q[:, c * N_KEYS:(c + 1) * N_KEYS], (((1,), (1,)), ((), ())),
            preferred_element_type=F32)


def _scores(u2, w_q, keys, tt=512):
    T = u2.shape[0]
    return pl.pallas_call(
        _score_kernel,
        grid=(T // tt,),
        in_specs=[pl.BlockSpec((tt, D), lambda i: (i, 0)),
                  pl.BlockSpec((D, D), lambda i: (0, 0)),
                  pl.BlockSpec((2 * N_HEADS, N_KEYS, N_KEYS), lambda i: (0, 0, 0))],
        out_specs=pl.BlockSpec((D, tt), lambda i: (0, i)),
        out_shape=jax.ShapeDtypeStruct((D, T), F32),
        compiler_params=_params("arbitrary"),
        name="peer_scores",
    )(u2, w_q, keys)


TOPK_SUB = 256


def _top_values(s):
    vals = []
    m = jnp.max(s, axis=0, keepdims=True)
    vals.append(m)
    for _ in range(TOPK - 1):
        m = jnp.max(jnp.where(s < m, s, NEG_BIG), axis=0, keepdims=True)
        vals.append(m)
    return vals


def _topk_kernel(st_ref, e1_ref, e2_ref, tau_ref, v1_ref, v2_ref, aux_ref, *, tl):
    n_grp = tl // LANES

    def head(h, carry):
        r1 = pl.multiple_of(h * 2 * N_KEYS, 2 * N_KEYS)
        r2 = r1 + N_KEYS
        for q in range(tl // TOPK_SUB):
            lo = q * TOPK_SUB
            for half, vref in ((r1, v1_ref), (r2, v2_ref)):
                vals = _top_values(st_ref[pl.ds(half, N_KEYS), lo:lo + TOPK_SUB])
                for r in range(TOPK):
                    for g in range(TOPK_SUB // LANES):
                        gg = q * (TOPK_SUB // LANES) + g
                        vref[r, gg:gg + 1, :] = vals[r][:, g * LANES:(g + 1) * LANES]
        v1 = [v1_ref[r] for r in range(TOPK)]
        v2 = [v2_ref[r] for r in range(TOPK)]
        cands = [v1[r] + v2[c] for (r, c) in _CAND]
        m = functools.reduce(jnp.maximum, cands)
        for _ in range(TOPK - 1):
            m = functools.reduce(jnp.maximum, [jnp.where(cd < m, cd, NEG_BIG) for cd in cands])
        tau = m
        top = cands[0]
        z = functools.reduce(
            lambda a, b: a + b, [jnp.where(cd >= tau, jnp.exp(cd - top), 0.0) for cd in cands])
        aux_ref[0] = tau
        aux_ref[1] = 1.0 / z
        for g in range(n_grp):
            ls = slice(g * LANES, (g + 1) * LANES)
            s1 = st_ref[pl.ds(r1, N_KEYS), ls]
            s2 = st_ref[pl.ds(r2, N_KEYS), ls]
            rz = aux_ref[1, g:g + 1, :]
            e1 = jnp.where(s1 >= v1_ref[TOPK - 1, g:g + 1, :], jnp.exp(s1 - v1_ref[0, g:g + 1, :]), 0.0)
            e2 = jnp.where(s2 >= v2_ref[TOPK - 1, g:g + 1, :], jnp.exp(s2 - v2_ref[0, g:g + 1, :]), 0.0)
            e1_ref[h, :, ls] = e1 * rz
            e2_ref[h, :, ls] = e2
            tau_ref[h, :, ls] = aux_ref[0, g:g + 1, :]
        return carry

    lax.fori_loop(0, N_HEADS, head, 0)


def _topk(st, tl=512):
    T = st.shape[1]
    n_grp = tl // LANES
    e_spec = pl.BlockSpec((N_HEADS, N_KEYS, tl), lambda i: (0, 0, i))
    return pl.pallas_call(
        functools.partial(_topk_kernel, tl=tl),
        grid=(T // tl,),
        in_specs=[pl.BlockSpec((D, tl), lambda i: (0, i))],
        out_specs=[e_spec, e_spec, pl.BlockSpec((N_HEADS, 1, tl), lambda i: (0, 0, i))],
        out_shape=[jax.ShapeDtypeStruct((N_HEADS, N_KEYS, T), F32),
                   jax.ShapeDtypeStruct((N_HEADS, N_KEYS, T), F32),
                   jax.ShapeDtypeStruct((N_HEADS, 1, T), F32)],
        scratch_shapes=[pltpu.VMEM((TOPK, n_grp, LANES), F32), pltpu.VMEM((TOPK, n_grp, LANES), F32),
                        pltpu.VMEM((2, n_grp, LANES), F32)],
        compiler_params=_params("arbitrary"),
        name="peer_topk",
    )(st)


PEER_TC = 256


def _peer_kernel(u2_ref, pu_ref, pvt_ref, s1_ref, s2_ref, e1_ref, e2_ref, tau_ref, o_ref, g_ref,
                 *, tt, te):
    k = pl.program_id(1)
    at = lax.dot_general(pu_ref[...], u2_ref[...], (((1,), (1,)), ((), ())),
                         preferred_element_type=F32)
    act = 0.5 * at * (1.0 + lax.erf(at * INV_SQRT2))
    n_i = te // N_KEYS
    for il in range(n_i):
        i = k * n_i + il
        rows = slice(il * N_KEYS, (il + 1) * N_KEYS)
        for t0 in range(0, tt, PEER_TC):
            ts = slice(t0, t0 + PEER_TC)
            w = jnp.zeros((N_KEYS, PEER_TC), F32)
            for h in range(N_HEADS):
                s1row = s1_ref[h, pl.ds(i, 1), ts]
                e1row = e1_ref[h, pl.ds(i, 1), ts]
                sel = (s1row + s2_ref[h, :, ts]) >= tau_ref[h, :, ts]
                w = w + jnp.where(sel, e2_ref[h, :, ts], 0.0) * e1row
            g_ref[rows, ts] = (w * act[rows, ts]).astype(BF16)
    contrib = jnp.dot(pvt_ref[...], g_ref[...], preferred_element_type=F32)

    @pl.when(k == 0)
    def _():
        o_ref[...] = contrib

    @pl.when(k > 0)
    def _():
        o_ref[...] += contrib


def _peer(u2, peer_u, peer_vt, st, e1, e2, tau, tt=512, te=512):
    T = u2.shape[0]
    n_exp = peer_u.shape[0]
    s4 = st.reshape(N_HEADS, 2, N_KEYS, T)
    s_spec = lambda p: pl.BlockSpec((N_HEADS, None, N_KEYS, tt), lambda i, k: (0, p, 0, i))
    e_spec = pl.BlockSpec((N_HEADS, N_KEYS, tt), lambda i, k: (0, 0, i))
    return pl.pallas_call(
        functools.partial(_peer_kernel, tt=tt, te=te),
        grid=(T // tt, n_exp // te),
        in_specs=[pl.BlockSpec((tt, D), lambda i, k: (i, 0)),
                  pl.BlockSpec((te, D), lambda i, k: (k, 0)),
                  pl.BlockSpec((D, te), lambda i, k: (0, k)),
                  s_spec(0), s_spec(1), e_spec, e_spec,
                  pl.BlockSpec((N_HEADS, 1, tt), lambda i, k: (0, 0, i))],
        out_specs=pl.BlockSpec((D, tt), lambda i, k: (0, i)),
        out_shape=jax.ShapeDtypeStruct((D, T), F32),
        scratch_shapes=[pltpu.VMEM((te, tt), BF16)],
        compiler_params=_params("arbitrary", "arbitrary"),
        name="peer_mix",
    )(u2, peer_u, peer_vt, s4, s4, e1, e2, tau)


def _final_kernel(ft_ref, x1_ref, g2_ref, lg_ref, lb_ref, y_ref, *, alpha):
    f = ft_ref[...].T
    h = alpha * x1_ref[...] + (1.0 + g2_ref[...]) * f
    y_ref[...] = _layernorm(h, lg_ref[...], lb_ref[...])


def _final(ft, col0, x1, mod, per_token, tiles_per_seq, lg, lb, alpha, tt=256):
    T = x1.shape[0]
    spec = _mod_spec_fn(per_token, tt, tiles_per_seq)
    off = col0 // tt
    row = pl.BlockSpec((tt, D), lambda i: (i, 0))
    const = pl.BlockSpec((1, D), lambda i: (0, 0))
    return pl.pallas_call(
        functools.partial(_final_kernel, alpha=alpha),
        grid=(T // tt,),
        in_specs=[pl.BlockSpec((D, tt), lambda i: (0, off + i)), row, spec(5), const, const],
        out_specs=row,
        out_shape=jax.ShapeDtypeStruct((T, D), F32),
        compiler_params=_params("arbitrary"),
        name="final_ln",
    )(ft, x1, mod, lg, lb)


def _pad_rows(w, n):
    return jnp.pad(w, ((0, n - w.shape[0]), (0, 0)))


def _layer(xp, xs, c_all, n_p, state_a, state_b, w_mod, b_mod, w_in, conv_a_w, conv_a_b, ln_a_g,
           ln_a_b, w_a_out, conv_b_w, w_b_out, w_o, ln1_g, ln1_b, w_q, sub_keys, peer_u, peer_v,
           ln2_g, ln2_b, alpha, seq_len, n_pos):
    n_s = xs.shape[0] // n_pos
    Tp = xp.shape[0]
    row = lambda v: v.reshape(1, D)

    mod = _modulation(c_all, w_mod, b_mod)
    mod_p = mod[:n_p].reshape(n_p, 6, 1, D)
    mod_s = jnp.tile(mod[n_p:n_p + n_s].reshape(n_s, 6, D).transpose(1, 0, 2), (1, n_pos, 1))

    w_in_b = w_in.astype(BF16)
    wa = _pad_rows(conv_a_w, 32)
    wb = _pad_rows(conv_b_w, 8)
    conv_args = (wa, row(conv_a_b), row(ln_a_g), row(ln_a_b), wb)

    glu_p, cgh_p, bg_p, sga_p, sgb_p = _in_proj(xp, mod_p, False, seq_len // 512, w_in_b)
    ya_p, yb_p = _conv_prompt(glu_p, cgh_p, bg_p, *conv_args, seq_len=seq_len)
    glu_s, cgh_s, bg_s, sga_s, sgb_s = _in_proj(xs, mod_s, True, 1, w_in_b)
    slab = lambda v: v.reshape(n_pos, n_s, D)
    ya_s, yb_s = _conv_sample(slab(glu_s), slab(cgh_s), slab(bg_s), state_a, state_b, *conv_args)

    proj_w = (w_a_out.astype(BF16), w_b_out.astype(BF16), w_o.astype(BF16), row(ln1_g), row(ln1_b))
    x1_p, u2_p = _out_proj(ya_p, yb_p, sga_p, sgb_p, xp, mod_p, False, seq_len // 512, *proj_w, alpha)
    x1_s, u2_s = _out_proj(ya_s.reshape(-1, D), yb_s.reshape(-1, D), sga_s, sgb_s, xs, mod_s, True, 1,
                           *proj_w, alpha)

    u2 = jnp.concatenate([u2_p, u2_s], axis=0)
    keys = sub_keys.astype(BF16).reshape(2 * N_HEADS, N_KEYS, N_KEYS)
    st = _scores(u2, w_q.astype(BF16), keys)
    e1, e2, tau = _topk(st)
    ft = _peer(u2, peer_u.astype(BF16), peer_v.T.astype(BF16), st, e1, e2, tau)

    y_p = _final(ft, 0, x1_p, mod_p, False, seq_len // 256, row(ln2_g), row(ln2_b), alpha)
    y_s = _final(ft, Tp, x1_s, mod_s, True, 1, row(ln2_g), row(ln2_b), alpha)

    new_a_p = glu_p.reshape(n_p, seq_len, D)[:, seq_len - (CONF_W - 1):]
    new_b_p = cgh_p.reshape(n_p, seq_len, D)[:, seq_len - (SC_W - 1):]
    glu_s_sm = slab(glu_s).transpose(1, 0, 2)
    cgh_s_sm = slab(cgh_s).transpose(1, 0, 2)
    new_a_s = jnp.concatenate([state_a, glu_s_sm], axis=1)[:, -(CONF_W - 1):]
    new_b_s = jnp.concatenate([state_b, cgh_s_sm], axis=1)[:, -(SC_W - 1):]
    return y_p, y_s, new_a_p, new_b_p, new_a_s, new_b_s


def kernel(x_prompt, x_sample, state_conv_a, state_conv_b, c_prompt, c_sample, w_mod, b_mod, w_in, conv_a_w, conv_a_b, ln_a_g, ln_a_b, w_a_out, conv_b_w, w_b_out, w_o, ln1_g, ln1_b, w_q, sub_keys, peer_u, peer_v, ln2_g, ln2_b):
    n_p, seq_len, d = x_prompt.shape
    n_s, n_pos, _ = x_sample.shape
    depth = w_mod.shape[0]
    assert d == D and seq_len % 512 == 0 and (n_s * n_pos) % 512 == 0 and n_s % 32 == 0
    alpha = (2.0 * depth) ** 0.25

    c_all = jnp.concatenate([c_prompt, c_sample], axis=0)
    c_all = _pad_rows(c_all, -(-c_all.shape[0] // 8) * 8)
    xp = x_prompt.reshape(n_p * seq_len, D)
    xs = x_sample.transpose(1, 0, 2).reshape(n_pos * n_s, D)

    layer_w = (w_mod, b_mod, w_in, conv_a_w, conv_a_b, ln_a_g, ln_a_b, w_a_out, conv_b_w, w_b_out,
               w_o, ln1_g, ln1_b, w_q, sub_keys, peer_u, peer_v, ln2_g, ln2_b)
    na_p, nb_p, na_s, nb_s = [], [], [], []
    for l in range(depth):
        xp, xs, a_p, b_p, a_s, b_s = _layer(
            xp, xs, c_all, n_p, state_conv_a[l], state_conv_b[l], *[w[l] for w in layer_w],
            alpha=alpha, seq_len=seq_len, n_pos=n_pos)
        na_p.append(a_p)
        nb_p.append(b_p)
        na_s.append(a_s)
        nb_s.append(b_s)
    y_prompt = xp.reshape(n_p, seq_len, D)
    y_sample = xs.reshape(n_pos, n_s, D).transpose(1, 0, 2)
    return (y_prompt, y_sample, jnp.stack(na_p), jnp.stack(nb_p), jnp.stack(na_s), jnp.stack(nb_s))
```

```python
import functools

import jax
import jax.numpy as jnp
from jax import lax
from jax.experimental import pallas as pl
from jax.experimental.pallas import tpu as pltpu

F32 = jnp.float32
BF16 = jnp.bfloat16

D = 2048
CONF_W = 31
SC_W = 3
N_HEADS = 8
N_KEYS = 128
TOPK = 16
LN_EPS = 1e-5
INV_SQRT2 = 0.7071067811865476
NEG_BIG = -3.0e38

LANES = 128
VMEM_LIMIT = 56 * 1024 * 1024

_CAND = tuple((r, c) for r in range(TOPK) for c in range(TOPK) if (r + 1) * (c + 1) <= TOPK)


def _params(*sem, flags=None):
    return pltpu.CompilerParams(dimension_semantics=sem, vmem_limit_bytes=VMEM_LIMIT, flags=flags)


def _layernorm(x, g, b):
    mu = jnp.mean(x, axis=-1, keepdims=True)
    xc = x - mu
    var = jnp.mean(xc * xc, axis=-1, keepdims=True)
    return xc * lax.rsqrt(var + LN_EPS) * g + b


def _mod_spec_fn(per_token, tt, tiles_per_seq):
    def spec(k):
        if per_token:
            return pl.BlockSpec((None, tt, D), lambda i, *_: (k, i, 0))
        return pl.BlockSpec((None, None, 1, D), lambda i, *_: (i // tiles_per_seq, k, 0, 0))
    return spec


def _mod_kernel(c_ref, w_ref, b_ref, o_ref):
    c = c_ref[...]
    s = (c * jax.nn.sigmoid(c)).astype(BF16)
    o_ref[...] = jnp.dot(s, w_ref[...].astype(BF16), preferred_element_type=F32) + b_ref[...]


def _modulation(c_all, w_mod, b_mod):
    n = c_all.shape[0]
    tn = 1024
    return pl.pallas_call(
        _mod_kernel,
        grid=(6 * D // tn,),
        in_specs=[pl.BlockSpec((n, D), lambda j: (0, 0)),
                  pl.BlockSpec((D, tn), lambda j: (0, j)),
                  pl.BlockSpec((1, tn), lambda j: (0, j))],
        out_specs=pl.BlockSpec((n, tn), lambda j: (0, j)),
        out_shape=jax.ShapeDtypeStruct((n, 6 * D), F32),
        compiler_params=_params("arbitrary"),
        name="modulation",
    )(c_all, w_mod, b_mod.reshape(1, 6 * D))


def _inproj_kernel(x_ref, sc_ref, sh_ref, w0, w1, w2, w3, w4, w5, w6,
                   glu_ref, cgh_ref, bg_ref, sga_ref, sgb_ref):
    u = (x_ref[...] * (1.0 + sc_ref[...]) + sh_ref[...]).astype(BF16)

    def z(w):
        return jnp.dot(u, w[...], preferred_element_type=F32)

    glu_ref[...] = z(w0) * jax.nn.sigmoid(z(w1))
    bg_ref[...] = z(w2)
    cgh_ref[...] = z(w3) * z(w4)
    sga_ref[...] = jax.nn.sigmoid(z(w5))
    sgb_ref[...] = jax.nn.sigmoid(z(w6))


def _in_proj(x2d, mod, per_token, tiles_per_seq, w_in, tt=512, tn=256):
    T = x2d.shape[0]
    spec = _mod_spec_fn(per_token, tt, tiles_per_seq)
    nj = D // tn
    w_specs = [pl.BlockSpec((D, tn), lambda i, j, g=g: (0, g * nj + j)) for g in range(7)]
    o_spec = pl.BlockSpec((tt, tn), lambda i, j: (i, j))
    return pl.pallas_call(
        _inproj_kernel,
        grid=(T // tt, nj),
        in_specs=[pl.BlockSpec((tt, D), lambda i, j: (i, 0)), spec(1), spec(0)] + w_specs,
        out_specs=[o_spec] * 5,
        out_shape=[jax.ShapeDtypeStruct((T, D), F32)] * 5,
        compiler_params=_params("arbitrary", "arbitrary"),
        name="in_proj",
    )(x2d, mod, mod, *([w_in] * 7))


CONV_CC = 256
CONV_RC = 64
HALO_A = 32
HALO_B = 8


def _conv_prompt_kernel(g_ref, gh_ref, c_ref, ch_ref, bg_ref, wa_ref, ba_ref, lg_ref, lb_ref,
                        wb_ref, ya_ref, yb_ref, ext_ref, extb_ref, pre_ref, *, tt, tiles_per_seq):
    i = pl.program_id(0)
    first = i % tiles_per_seq == 0

    @pl.when(first)
    def _():
        ext_ref[0:HALO_A, :] = jnp.zeros((HALO_A, D), F32)
        extb_ref[0:HALO_B, :] = jnp.zeros((HALO_B, D), F32)

    @pl.when(jnp.logical_not(first))
    def _():
        ext_ref[0:HALO_A, :] = gh_ref[...]
        extb_ref[0:HALO_B, :] = ch_ref[...]

    ext_ref[HALO_A:, :] = g_ref[...]
    extb_ref[HALO_B:, :] = c_ref[...]

    def chunk(c, carry):
        cs = pl.ds(pl.multiple_of(c * CONV_CC, CONV_CC), CONV_CC)
        wa = wa_ref[:, cs]
        wb = wb_ref[:, cs]
        bias = jnp.broadcast_to(ba_ref[:, cs], (CONV_RC, CONV_CC))
        for r in range(tt // CONV_RC):
            r0 = r * CONV_RC
            acc = bias
            lead = HALO_A - (CONF_W - 1)
            for b in range(8):
                taps = [k for k in range(CONF_W) if (lead + k) % 8 == b]
                span = (lead + taps[-1]) // 8 * 8 + CONV_RC
                win = ext_ref[pl.ds(r0 + b, span), cs]
                for k in taps:
                    a8 = (lead + k) // 8 * 8
                    acc = acc + win[a8:a8 + CONV_RC, :] * wa[k:k + 1, :]
            pre_ref[pl.ds(r0, CONV_RC), cs] = acc
            accb = extb_ref[pl.ds(r0 + HALO_B - (SC_W - 1), CONV_RC), cs] * wb[0:1, :]
            for k in range(1, SC_W):
                accb = accb + extb_ref[pl.ds(r0 + HALO_B - (SC_W - 1) + k, CONV_RC), cs] * wb[k:k + 1, :]
            yb_ref[pl.ds(r0, CONV_RC), cs] = (bg_ref[pl.ds(r0, CONV_RC), cs] * accb).astype(BF16)
        return carry

    lax.fori_loop(0, D // CONV_CC, chunk, 0)
    y = _layernorm(pre_ref[...], lg_ref[...], lb_ref[...])
    ya_ref[...] = (y * jax.nn.sigmoid(y)).astype(BF16)


def _conv_prompt(glu, cgh, bg, wa, ba, lg, lb, wb, seq_len, tt=256):
    T = glu.shape[0]
    tiles_per_seq = seq_len // tt
    row = pl.BlockSpec((tt, D), lambda i: (i, 0))
    const = lambda shape: pl.BlockSpec(shape, lambda i: (0, 0))
    halo_a = pl.BlockSpec((HALO_A, D), lambda i: (jnp.maximum(i * (tt // HALO_A) - 1, 0), 0))
    halo_b = pl.BlockSpec((HALO_B, D), lambda i: (jnp.maximum(i * (tt // HALO_B) - 1, 0), 0))
    return pl.pallas_call(
        functools.partial(_conv_prompt_kernel, tt=tt, tiles_per_seq=tiles_per_seq),
        grid=(T // tt,),
        in_specs=[row, halo_a, row, halo_b, row, const((32, D)), const((1, D)), const((1, D)),
                  const((1, D)), const((8, D))],
        out_specs=[row, row],
        out_shape=[jax.ShapeDtypeStruct((T, D), BF16)] * 2,
        scratch_shapes=[pltpu.VMEM((tt + HALO_A, D), F32), pltpu.VMEM((tt + HALO_B, D), F32),
                        pltpu.VMEM((tt, D), F32)],
        compiler_params=_params("arbitrary"),
        name="conv_prompt",
    )(glu, glu, cgh, cgh, bg, wa, ba, lg, lb, wb)


def _conv_sample_kernel(g_ref, sa_ref, c_ref, sb_ref, bg_ref, wa_ref, ba_ref, lg_ref, lb_ref,
                        wb_ref, ya_ref, yb_ref, pre_ref, *, n_pos, bs):
    n_hist_a = CONF_W - 1
    n_hist_b = SC_W - 1

    def chunk(c, carry):
        cs = pl.ds(pl.multiple_of(c * CONV_CC, CONV_CC), CONV_CC)
        wa = wa_ref[:, cs]
        wb = wb_ref[:, cs]
        bias = jnp.broadcast_to(ba_ref[:, cs], (bs, CONV_CC))
        acc = [bias] * n_pos
        for j in range(n_hist_a + n_pos):
            row = sa_ref[j, :, cs] if j < n_hist_a else g_ref[j - n_hist_a, :, cs]
            for p in range(n_pos):
                k = j - p
                if 0 <= k < CONF_W:
                    acc[p] = acc[p] + row * wa[k:k + 1, :]
        for p in range(n_pos):
            pre_ref[p, :, cs] = acc[p]
        rows_b = [sb_ref[j, :, cs] for j in range(n_hist_b)] + [c_ref[p, :, cs] for p in range(n_pos)]
        for p in range(n_pos):
            accb = rows_b[p] * wb[0:1, :]
            for k in range(1, SC_W):
                accb = accb + rows_b[p + k] * wb[k:k + 1, :]
            yb_ref[p, :, cs] = (bg_ref[p, :, cs] * accb).astype(BF16)
        return carry

    lax.fori_loop(0, D // CONV_CC, chunk, 0)
    for p in range(n_pos):
        y = _layernorm(pre_ref[p], lg_ref[...], lb_ref[...])
        ya_ref[p] = (y * jax.nn.sigmoid(y)).astype(BF16)


def _conv_sample(glu, cgh, bg, state_a, state_b, wa, ba, lg, lb, wb, bs=32):
    n_pos, n_seq, _ = glu.shape
    slab = pl.BlockSpec((n_pos, bs, D), lambda i: (0, i, 0))
    const = lambda shape: pl.BlockSpec(shape, lambda i: (0, 0))
    return pl.pallas_call(
        functools.partial(_conv_sample_kernel, n_pos=n_pos, bs=bs),
        grid=(n_seq // bs,),
        in_specs=[slab, pl.BlockSpec((CONF_W - 1, bs, D), lambda i: (0, i, 0)),
                  slab, pl.BlockSpec((SC_W - 1, bs, D), lambda i: (0, i, 0)),
                  slab, const((32, D)), const((1, D)), const((1, D)), const((1, D)), const((8, D))],
        out_specs=[slab, slab],
        out_shape=[jax.ShapeDtypeStruct((n_pos, n_seq, D), BF16)] * 2,
        scratch_shapes=[pltpu.VMEM((n_pos, bs, D), F32)],
        compiler_params=_params("arbitrary"),
        name="conv_sample",
    )(glu, state_a, cgh, state_b, bg, wa, ba, lg, lb, wb)


def _branch_kernel(ya_ref, yb_ref, sga_ref, sgb_ref, wa_ref, wb_ref, mm_ref):
    ya = jnp.dot(ya_ref[...], wa_ref[...], preferred_element_type=F32)
    yb = jnp.dot(yb_ref[...], wb_ref[...], preferred_element_type=F32)
    mm_ref[...] = (sga_ref[...] * ya + sgb_ref[...] * yb).astype(BF16)


def _ln1_kernel(mm_ref, x_ref, g1_ref, sc2_ref, sh2_ref, wo_ref, lg_ref, lb_ref, x1_ref, u2_ref,
                *, alpha):
    m = jnp.dot(mm_ref[...], wo_ref[...], preferred_element_type=F32)
    h = alpha * x_ref[...] + (1.0 + g1_ref[...]) * m
    x1 = _layernorm(h, lg_ref[...], lb_ref[...])
    x1_ref[...] = x1
    u2_ref[...] = (x1 * (1.0 + sc2_ref[...]) + sh2_ref[...]).astype(BF16)


def _out_proj(ya_in, yb_in, sga, sgb, x2d, mod, per_token, tiles_per_seq, w_a, w_b, w_o,
              lg, lb, alpha, tt=512, tn=512):
    T = x2d.shape[0]
    row2 = pl.BlockSpec((tt, D), lambda i, j: (i, 0))
    col2 = pl.BlockSpec((tt, tn), lambda i, j: (i, j))
    wcol = pl.BlockSpec((D, tn), lambda i, j: (0, j))
    mm = pl.pallas_call(
        _branch_kernel,
        grid=(T // tt, D // tn),
        in_specs=[row2, row2, col2, col2, wcol, wcol],
        out_specs=col2,
        out_shape=jax.ShapeDtypeStruct((T, D), BF16),
        compiler_params=_params("arbitrary", "arbitrary"),
        name="branch_proj",
    )(ya_in, yb_in, sga, sgb, w_a, w_b)

    spec = _mod_spec_fn(per_token, tt, tiles_per_seq)
    row = pl.BlockSpec((tt, D), lambda i: (i, 0))
    const = pl.BlockSpec((1, D), lambda i: (0, 0))
    return pl.pallas_call(
        functools.partial(_ln1_kernel, alpha=alpha),
        grid=(T // tt,),
        in_specs=[row, row, spec(2), spec(4), spec(3), pl.BlockSpec((D, D), lambda i: (0, 0)),
                  const, const],
        out_specs=[row, row],
        out_shape=[jax.ShapeDtypeStruct((T, D), F32), jax.ShapeDtypeStruct((T, D), BF16)],
        compiler_params=_params("arbitrary"),
        name="out_proj_ln1",
    )(mm, x2d, mod, mod, mod, w_o, lg, lb)


def _score_kernel(u2_ref, wq_ref, keys_ref, st_ref):
    q = jnp.dot(u2_ref[...], wq_ref[...], preferred_element_type=F32).astype(BF16)
    for c in range(2 * N_HEADS):
        st_ref[c * N_KEYS:(c + 1) * N_KEYS, :] = lax.dot_general(
            keys_ref[c], q[:, c * N_KEYS:(c + 1) * N_KEYS], (((1,), (1,)), ((), ())),
            preferred_element_type=F32)


def _scores(u2, w_q, keys, tt=512):
    T = u2.shape[0]
    return pl.pallas_call(
        _score_kernel,
        grid=(T // tt,),
        in_specs=[pl.BlockSpec((tt, D), lambda i: (i, 0)),
                  pl.BlockSpec((D, D), lambda i: (0, 0)),
                  pl.BlockSpec((2 * N_HEADS, N_KEYS, N_KEYS), lambda i: (0, 0, 0))],
        out_specs=pl.BlockSpec((D, tt), lambda i: (0, i)),
        out_shape=jax.ShapeDtypeStruct((D, T), F32),
        compiler_params=_params("arbitrary"),
        name="peer_scores",
    )(u2, w_q, keys)


TOPK_SUB = 256


def _top_values(s):
    vals = []
    m = jnp.max(s, axis=0, keepdims=True)
    vals.append(m)
    for _ in range(TOPK - 1):
        m = jnp.max(jnp.where(s < m, s, NEG_BIG), axis=0, keepdims=True)
        vals.append(m)
    return vals


def _topk_kernel(st_ref, e1_ref, e2_ref, phi_ref, v1_ref, v2_ref, ev2_ref, aux_ref, *, tl):
    n_grp = tl // LANES
    big = -NEG_BIG
    c_any = TOPK // 2

    def head(h, carry):
        r1 = pl.multiple_of(h * 2 * N_KEYS, 2 * N_KEYS)
        r2 = r1 + N_KEYS
        for q in range(tl // TOPK_SUB):
            lo = q * TOPK_SUB
            for half, vref in ((r1, v1_ref), (r2, v2_ref)):
                vals = _top_values(st_ref[pl.ds(half, N_KEYS), lo:lo + TOPK_SUB])
                for r in range(TOPK):
                    for g in range(TOPK_SUB // LANES):
                        gg = q * (TOPK_SUB // LANES) + g
                        vref[r, gg:gg + 1, :] = vals[r][:, g * LANES:(g + 1) * LANES]
        v1 = [v1_ref[r] for r in range(TOPK)]
        v2 = [v2_ref[r] for r in range(TOPK)]
        cands = [v1[r] + v2[c] for (r, c) in _CAND]
        m = functools.reduce(jnp.maximum, cands)
        for _ in range(TOPK - 1):
            m = functools.reduce(jnp.maximum, [jnp.where(cd < m, cd, NEG_BIG) for cd in cands])
        tau = m
        top = cands[0]
        z = functools.reduce(
            lambda a, b: a + b, [jnp.where(cd >= tau, jnp.exp(cd - top), 0.0) for cd in cands])
        ev2 = [jnp.exp(v2[c] - v2[0]) for c in range(TOPK)]
        for c in range(TOPK):
            ev2_ref[c] = ev2[c]
        phi_top = functools.reduce(
            jnp.minimum, [jnp.where(v1[0] + v2[c] >= tau, ev2[c], big) for c in range(c_any, TOPK)])
        aux_ref[0] = tau
        aux_ref[1] = 1.0 / z
        aux_ref[2] = phi_top
        for g in range(n_grp):
            ls = slice(g * LANES, (g + 1) * LANES)
            row = lambda ref, r: ref[r, g:g + 1, :]
            s1 = st_ref[pl.ds(r1, N_KEYS), ls]
            s2 = st_ref[pl.ds(r2, N_KEYS), ls]
            tau_row = row(aux_ref, 0)
            phi = jnp.where(s1 + row(v2_ref, 0) >= tau_row, row(ev2_ref, 0), big)
            for c in range(1, c_any):
                phi = jnp.minimum(phi, jnp.where(s1 + row(v2_ref, c) >= tau_row, row(ev2_ref, c), big))
            phi = jnp.where(s1 >= row(v1_ref, 0), jnp.minimum(phi, row(aux_ref, 2)), phi)
            e1 = jnp.where(s1 >= row(v1_ref, TOPK - 1), jnp.exp(s1 - row(v1_ref, 0)), 0.0)
            e2 = jnp.where(s2 >= row(v2_ref, TOPK - 1), jnp.exp(s2 - row(v2_ref, 0)), 0.0)
            e1_ref[h, :, ls] = e1 * row(aux_ref, 1)
            e2_ref[h, :, ls] = e2
            phi_ref[h, :, ls] = phi
        return carry

    lax.fori_loop(0, N_HEADS, head, 0)


def _topk(st, tl=512):
    T = st.shape[1]
    n_grp = tl // LANES
    e_spec = pl.BlockSpec((N_HEADS, N_KEYS, tl), lambda i: (0, 0, i))
    return pl.pallas_call(
        functools.partial(_topk_kernel, tl=tl),
        grid=(T // tl,),
        in_specs=[pl.BlockSpec((D, tl), lambda i: (0, i))],
        out_specs=[e_spec, e_spec, e_spec],
        out_shape=[jax.ShapeDtypeStruct((N_HEADS, N_KEYS, T), F32)] * 3,
        scratch_shapes=[pltpu.VMEM((TOPK, n_grp, LANES), F32), pltpu.VMEM((TOPK, n_grp, LANES), F32),
                        pltpu.VMEM((TOPK, n_grp, LANES), F32), pltpu.VMEM((3, n_grp, LANES), F32)],
        compiler_params=_params("arbitrary"),
        name="peer_topk",
    )(st)


PEER_TC = 256


PEER_IROWS = 8


def _peer_gate_tile(row, col, act_ref, e1_ref, phi_ref, e2_ref, g_ref, *, tt):
    cols = slice(col * N_KEYS, (col + 1) * N_KEYS)
    for t0 in range(0, tt, PEER_TC):
        ts = slice(t0, t0 + PEER_TC)
        w = jnp.zeros((N_KEYS, PEER_TC), F32)
        for h in range(N_HEADS):
            e1row = e1_ref[h, row:row + 1, ts]
            phirow = phi_ref[h, row:row + 1, ts]
            e2 = e2_ref[h, :, ts]
            w = w + jnp.where(e2 >= phirow, e2, 0.0) * e1row
        g_ref[ts, cols] = (w.T * act_ref[ts, cols]).astype(BF16)


def _peer_kernel(u2_ref, pu_ref, pun_ref, pv_ref, pvb_ref, e1_ref, phi_ref, e2_ref, o_ref,
                 acta_ref, actb_ref, ga_ref, gb_ref, ob_ref, *, tt, te, n_k):
    k = pl.program_id(1)
    half = te // 2
    n_il = half // N_KEYS
    sub = half // 2
    wide = D // 2

    def act_of(rows):
        a = lax.dot_general(u2_ref[...], rows, (((1,), (1,)), ((), ())), preferred_element_type=F32)
        return 0.5 * a * (1.0 + lax.erf(a * INV_SQRT2))

    @pl.when(k == 0)
    def _():
        o_ref[...] = jnp.zeros((tt, D), F32)
        ob_ref[...] = jnp.zeros((tt, D), F32)
        gb_ref[...] = jnp.zeros((tt, half), BF16)
        acta_ref[...] = act_of(pu_ref[0:half, :])

    def act_job(dst_ref, src_ref, r0, j):
        def run():
            dst_ref[:, j * sub:(j + 1) * sub] = act_of(src_ref[r0 + j * sub:r0 + (j + 1) * sub, :])
        return run

    def mix_job(acc_ref, g_ref, v_ref, j):
        def run():
            cs = slice(j * wide, (j + 1) * wide)
            acc_ref[:, cs] += jnp.dot(g_ref[...], v_ref[0:half, cs], preferred_element_type=F32)
        return run

    jobs_a = [act_job(actb_ref, pu_ref, half, 0), mix_job(ob_ref, gb_ref, pvb_ref, 0),
              act_job(actb_ref, pu_ref, half, 1), mix_job(ob_ref, gb_ref, pvb_ref, 1)]
    jobs_b = [act_job(acta_ref, pun_ref, 0, 0), mix_job(o_ref, ga_ref, pv_ref, 0),
              act_job(acta_ref, pun_ref, 0, 1), mix_job(o_ref, ga_ref, pv_ref, 1)]
    gate = functools.partial(_peer_gate_tile, e1_ref=e1_ref, phi_ref=phi_ref, e2_ref=e2_ref, tt=tt)
    for il in range(n_il):
        jobs_a[il]()
        gate(il, il, acta_ref, g_ref=ga_ref)
    for il in range(n_il):
        jobs_b[il]()
        gate(n_il + il, il, actb_ref, g_ref=gb_ref)

    @pl.when(k == n_k - 1)
    def _():
        o_ref[...] += ob_ref[...] + jnp.dot(gb_ref[...], pv_ref[half:te, :],
                                            preferred_element_type=F32)


def _peer(u2, peer_u, peer_v, e1, phi, e2, tt=512, te=1024):
    T = u2.shape[0]
    n_exp = peer_u.shape[0]
    assert te == PEER_IROWS * N_KEYS and n_exp % te == 0
    n_k = n_exp // te
    half = te // 2
    e1r = e1.reshape(N_HEADS, n_k, PEER_IROWS, T)
    phir = phi.reshape(N_HEADS, n_k, PEER_IROWS, T)
    rows_spec = pl.BlockSpec((N_HEADS, None, PEER_IROWS, tt), lambda i, k: (0, k, 0, i))
    tile = pl.BlockSpec((te, D), lambda i, k: (k, 0))
    return pl.pallas_call(
        functools.partial(_peer_kernel, tt=tt, te=te, n_k=n_k),
        grid=(T // tt, n_k),
        in_specs=[pl.BlockSpec((tt, D), lambda i, k: (i, 0)),
                  tile,
                  pl.BlockSpec((half, D), lambda i, k: (jnp.minimum(2 * k + 2, 2 * n_k - 2), 0)),
                  tile,
                  pl.BlockSpec((half, D), lambda i, k: (jnp.maximum(2 * k - 1, 0), 0)),
                  rows_spec, rows_spec,
                  pl.BlockSpec((N_HEADS, N_KEYS, tt), lambda i, k: (0, 0, i))],
        out_specs=pl.BlockSpec((tt, D), lambda i, k: (i, 0)),
        out_shape=jax.ShapeDtypeStruct((T, D), F32),
        scratch_shapes=[pltpu.VMEM((tt, half), F32), pltpu.VMEM((tt, half), F32),
                        pltpu.VMEM((tt, half), BF16), pltpu.VMEM((tt, half), BF16),
                        pltpu.VMEM((tt, D), F32)],
        compiler_params=_params("arbitrary", "arbitrary"),
        name="peer_mix",
    )(u2, peer_u, peer_u, peer_v, peer_v, e1r, phir, e2)


def _final_kernel(f_ref, x1_ref, g2_ref, lg_ref, lb_ref, y_ref, *, alpha):
    h = alpha * x1_ref[...] + (1.0 + g2_ref[...]) * f_ref[...]
    y_ref[...] = _layernorm(h, lg_ref[...], lb_ref[...])


def _final(f, row0, x1, mod, per_token, tiles_per_seq, lg, lb, alpha, tt=256):
    T = x1.shape[0]
    spec = _mod_spec_fn(per_token, tt, tiles_per_seq)
    off = row0 // tt
    row = pl.BlockSpec((tt, D), lambda i: (i, 0))
    const = pl.BlockSpec((1, D), lambda i: (0, 0))
    return pl.pallas_call(
        functools.partial(_final_kernel, alpha=alpha),
        grid=(T // tt,),
        in_specs=[pl.BlockSpec((tt, D), lambda i: (off + i, 0)), row, spec(5), const, const],
        out_specs=row,
        out_shape=jax.ShapeDtypeStruct((T, D), F32),
        compiler_params=_params("arbitrary"),
        name="final_ln",
    )(f, x1, mod, lg, lb)


def _pad_rows(w, n):
    return jnp.pad(w, ((0, n - w.shape[0]), (0, 0)))


def _layer(xp, xs, c_all, n_p, state_a, state_b, w_mod, b_mod, w_in, conv_a_w, conv_a_b, ln_a_g,
           ln_a_b, w_a_out, conv_b_w, w_b_out, w_o, ln1_g, ln1_b, w_q, sub_keys, peer_u, peer_v,
           ln2_g, ln2_b, alpha, seq_len, n_pos):
    n_s = xs.shape[0] // n_pos
    Tp = xp.shape[0]
    row = lambda v: v.reshape(1, D)

    mod = _modulation(c_all, w_mod, b_mod)
    mod_p = mod[:n_p].reshape(n_p, 6, 1, D)
    mod_s = jnp.tile(mod[n_p:n_p + n_s].reshape(n_s, 6, D).transpose(1, 0, 2), (1, n_pos, 1))

    w_in_b = w_in.astype(BF16)
    wa = _pad_rows(conv_a_w, 32)
    wb = _pad_rows(conv_b_w, 8)
    conv_args = (wa, row(conv_a_b), row(ln_a_g), row(ln_a_b), wb)

    glu_p, cgh_p, bg_p, sga_p, sgb_p = _in_proj(xp, mod_p, False, seq_len // 512, w_in_b)
    ya_p, yb_p = _conv_prompt(glu_p, cgh_p, bg_p, *conv_args, seq_len=seq_len)
    glu_s, cgh_s, bg_s, sga_s, sgb_s = _in_proj(xs, mod_s, True, 1, w_in_b)
    slab = lambda v: v.reshape(n_pos, n_s, D)
    ya_s, yb_s = _conv_sample(slab(glu_s), slab(cgh_s), slab(bg_s), state_a.transpose(1, 0, 2),
                              state_b.transpose(1, 0, 2), *conv_args)

    proj_w = (w_a_out.astype(BF16), w_b_out.astype(BF16), w_o.astype(BF16), row(ln1_g), row(ln1_b))
    x1_p, u2_p = _out_proj(ya_p, yb_p, sga_p, sgb_p, xp, mod_p, False, seq_len // 512, *proj_w, alpha)
    x1_s, u2_s = _out_proj(ya_s.reshape(-1, D), yb_s.reshape(-1, D), sga_s, sgb_s, xs, mod_s, True, 1,
                           *proj_w, alpha)

    u2 = jnp.concatenate([u2_p, u2_s], axis=0)
    keys = sub_keys.astype(BF16).reshape(2 * N_HEADS, N_KEYS, N_KEYS)
    st = _scores(u2, w_q.astype(BF16), keys)
    e1, e2, phi = _topk(st)
    f = _peer(u2, peer_u.astype(BF16), peer_v.astype(BF16), e1, phi, e2)

    y_p = _final(f, 0, x1_p, mod_p, False, seq_len // 256, row(ln2_g), row(ln2_b), alpha)
    y_s = _final(f, Tp, x1_s, mod_s, True, 1, row(ln2_g), row(ln2_b), alpha)

    new_a_p = glu_p.reshape(n_p, seq_len, D)[:, seq_len - (CONF_W - 1):]
    new_b_p = cgh_p.reshape(n_p, seq_len, D)[:, seq_len - (SC_W - 1):]
    glu_s_sm = slab(glu_s).transpose(1, 0, 2)
    cgh_s_sm = slab(cgh_s).transpose(1, 0, 2)
    new_a_s = jnp.concatenate([state_a, glu_s_sm], axis=1)[:, -(CONF_W - 1):]
    new_b_s = jnp.concatenate([state_b, cgh_s_sm], axis=1)[:, -(SC_W - 1):]
    return y_p, y_s, new_a_p, new_b_p, new_a_s, new_b_s


def kernel(x_prompt, x_sample, state_conv_a, state_conv_b, c_prompt, c_sample, w_mod, b_mod, w_in, conv_a_w, conv_a_b, ln_a_g, ln_a_b, w_a_out, conv_b_w, w_b_out, w_o, ln1_g, ln1_b, w_q, sub_keys, peer_u, peer_v, ln2_g, ln2_b):
    n_p, seq_len, d = x_prompt.shape
    n_s, n_pos, _ = x_sample.shape
    depth = w_mod.shape[0]
    assert d == D and seq_len % 512 == 0 and (n_s * n_pos) % 512 == 0 and n_s % 32 == 0
    alpha = (2.0 * depth) ** 0.25

    c_all = jnp.concatenate([c_prompt, c_sample], axis=0)
    c_all = _pad_rows(c_all, -(-c_all.shape[0] // 8) * 8)
    xp = x_prompt.reshape(n_p * seq_len, D)
    xs = x_sample.transpose(1, 0, 2).reshape(n_pos * n_s, D)

    layer_w = (w_mod, b_mod, w_in, conv_a_w, conv_a_b, ln_a_g, ln_a_b, w_a_out, conv_b_w, w_b_out,
               w_o, ln1_g, ln1_b, w_q, sub_keys, peer_u, peer_v, ln2_g, ln2_b)
    na_p, nb_p, na_s, nb_s = [], [], [], []
    for l in range(depth):
        xp, xs, a_p, b_p, a_s, b_s = _layer(
            xp, xs, c_all, n_p, state_conv_a[l], state_conv_b[l], *[w[l] for w in layer_w],
            alpha=alpha, seq_len=seq_len, n_pos=n_pos)
        na_p.append(a_p)
        nb_p.append(b_p)
        na_s.append(a_s)
        nb_s.append(b_s)
    y_prompt = xp.reshape(n_p, seq_len, D)
    y_sample = xs.reshape(n_pos, n_s, D).transpose(1, 0, 2)
    return (y_prompt, y_sample, jnp.stack(na_p), jnp.stack(nb_p), jnp.stack(na_s), jnp.stack(nb_s))
```

```python
import functools

import jax
import jax.numpy as jnp
from jax import lax
from jax.experimental import pallas as pl
from jax.experimental.pallas import tpu as pltpu

F32 = jnp.float32
BF16 = jnp.bfloat16

D = 2048
CONF_W = 31
SC_W = 3
N_HEADS = 8
N_KEYS = 128
TOPK = 16
LN_EPS = 1e-5
INV_SQRT2 = 0.7071067811865476
NEG_BIG = -3.0e38

LANES = 128
VMEM_LIMIT = 56 * 1024 * 1024

_CAND = tuple((r, c) for r in range(TOPK) for c in range(TOPK) if (r + 1) * (c + 1) <= TOPK)


def _params(*sem, flags=None):
    return pltpu.CompilerParams(dimension_semantics=sem, vmem_limit_bytes=VMEM_LIMIT, flags=flags)


def _layernorm(x, g, b):
    mu = jnp.mean(x, axis=-1, keepdims=True)
    xc = x - mu
    var = jnp.mean(xc * xc, axis=-1, keepdims=True)
    return xc * lax.rsqrt(var + LN_EPS) * g + b


def _mod_spec_fn(per_token, tt, tiles_per_seq):
    def spec(k):
        if per_token:
            return pl.BlockSpec((None, tt, D), lambda i, *_: (k, i, 0))
        return pl.BlockSpec((None, None, 1, D), lambda i, *_: (i // tiles_per_seq, k, 0, 0))
    return spec


def _mod_kernel(c_ref, w_ref, b_ref, o_ref):
    c = c_ref[...]
    s = (c * jax.nn.sigmoid(c)).astype(BF16)
    o_ref[...] = jnp.dot(s, w_ref[...].astype(BF16), preferred_element_type=F32) + b_ref[...]


def _modulation(c_all, w_mod, b_mod):
    n = c_all.shape[0]
    tn = 1024
    return pl.pallas_call(
        _mod_kernel,
        grid=(6 * D // tn,),
        in_specs=[pl.BlockSpec((n, D), lambda j: (0, 0)),
                  pl.BlockSpec((D, tn), lambda j: (0, j)),
                  pl.BlockSpec((1, tn), lambda j: (0, j))],
        out_specs=pl.BlockSpec((n, tn), lambda j: (0, j)),
        out_shape=jax.ShapeDtypeStruct((n, 6 * D), F32),
        compiler_params=_params("arbitrary"),
        name="modulation",
    )(c_all, w_mod, b_mod.reshape(1, 6 * D))


def _inproj_kernel(x_ref, sc_ref, sh_ref, w0, w1, w2, w3, w4, w5, w6,
                   glu_ref, cgh_ref, bg_ref, sga_ref, sgb_ref):
    u = (x_ref[...] * (1.0 + sc_ref[...]) + sh_ref[...]).astype(BF16)

    def z(w):
        return jnp.dot(u, w[...], preferred_element_type=F32)

    glu_ref[...] = z(w0) * jax.nn.sigmoid(z(w1))
    bg_ref[...] = z(w2)
    cgh_ref[...] = z(w3) * z(w4)
    sga_ref[...] = jax.nn.sigmoid(z(w5))
    sgb_ref[...] = jax.nn.sigmoid(z(w6))


def _in_proj(x2d, mod, per_token, tiles_per_seq, w_in, tt=512, tn=256):
    T = x2d.shape[0]
    spec = _mod_spec_fn(per_token, tt, tiles_per_seq)
    nj = D // tn
    w_specs = [pl.BlockSpec((D, tn), lambda i, j, g=g: (0, g * nj + j)) for g in range(7)]
    o_spec = pl.BlockSpec((tt, tn), lambda i, j: (i, j))
    return pl.pallas_call(
        _inproj_kernel,
        grid=(T // tt, nj),
        in_specs=[pl.BlockSpec((tt, D), lambda i, j: (i, 0)), spec(1), spec(0)] + w_specs,
        out_specs=[o_spec] * 5,
        out_shape=[jax.ShapeDtypeStruct((T, D), F32)] * 5,
        compiler_params=_params("arbitrary", "arbitrary"),
        name="in_proj",
    )(x2d, mod, mod, *([w_in] * 7))


CONV_CC = 256
CONV_RC = 64
HALO_A = 32
HALO_B = 8


def _conv_prompt_kernel(g_ref, gh_ref, c_ref, ch_ref, bg_ref, wa_ref, ba_ref, lg_ref, lb_ref,
                        wb_ref, ya_ref, yb_ref, ext_ref, extb_ref, pre_ref, sh_ref, *, tt, tiles_per_seq):
    i = pl.program_id(0)
    first = i % tiles_per_seq == 0

    @pl.when(first)
    def _():
        ext_ref[0:HALO_A, :] = jnp.zeros((HALO_A, D), F32)
        extb_ref[0:HALO_B, :] = jnp.zeros((HALO_B, D), F32)

    @pl.when(jnp.logical_not(first))
    def _():
        ext_ref[0:HALO_A, :] = gh_ref[...]
        extb_ref[0:HALO_B, :] = ch_ref[...]

    ext_ref[HALO_A:, :] = g_ref[...]
    extb_ref[HALO_B:, :] = c_ref[...]

    def chunk(c, carry):
        cs = pl.ds(pl.multiple_of(c * CONV_CC, CONV_CC), CONV_CC)
        wa = wa_ref[:, cs]
        wb = wb_ref[:, cs]
        bias = jnp.broadcast_to(ba_ref[:, cs], (CONV_RC, CONV_CC))
        for r in range(tt // CONV_RC):
            r0 = r * CONV_RC
            acc = bias
            lead = HALO_A - (CONF_W - 1)
            for b in range(8):
                taps = [k for k in range(CONF_W) if (lead + k) % 8 == b]
                span = (lead + taps[-1]) // 8 * 8 + CONV_RC
                if b:
                    sh_ref[b - 1, 0:span, :] = ext_ref[pl.ds(r0 + b, span), cs]
                for k in taps:
                    a8 = (lead + k) // 8 * 8
                    win = (sh_ref[b - 1, a8:a8 + CONV_RC, :] if b
                           else ext_ref[pl.ds(r0 + a8, CONV_RC), cs])
                    acc = acc + win * wa[k:k + 1, :]
            pre_ref[pl.ds(r0, CONV_RC), cs] = acc
            accb = extb_ref[pl.ds(r0 + HALO_B - (SC_W - 1), CONV_RC), cs] * wb[0:1, :]
            for k in range(1, SC_W):
                accb = accb + extb_ref[pl.ds(r0 + HALO_B - (SC_W - 1) + k, CONV_RC), cs] * wb[k:k + 1, :]
            yb_ref[pl.ds(r0, CONV_RC), cs] = (bg_ref[pl.ds(r0, CONV_RC), cs] * accb).astype(BF16)
        return carry

    lax.fori_loop(0, D // CONV_CC, chunk, 0)
    y = _layernorm(pre_ref[...], lg_ref[...], lb_ref[...])
    ya_ref[...] = (y * jax.nn.sigmoid(y)).astype(BF16)


def _conv_prompt(glu, cgh, bg, wa, ba, lg, lb, wb, seq_len, tt=256):
    T = glu.shape[0]
    tiles_per_seq = seq_len // tt
    row = pl.BlockSpec((tt, D), lambda i: (i, 0))
    const = lambda shape: pl.BlockSpec(shape, lambda i: (0, 0))
    halo_a = pl.BlockSpec((HALO_A, D), lambda i: (jnp.maximum(i * (tt // HALO_A) - 1, 0), 0))
    halo_b = pl.BlockSpec((HALO_B, D), lambda i: (jnp.maximum(i * (tt // HALO_B) - 1, 0), 0))
    return pl.pallas_call(
        functools.partial(_conv_prompt_kernel, tt=tt, tiles_per_seq=tiles_per_seq),
        grid=(T // tt,),
        in_specs=[row, halo_a, row, halo_b, row, const((32, D)), const((1, D)), const((1, D)),
                  const((1, D)), const((8, D))],
        out_specs=[row, row],
        out_shape=[jax.ShapeDtypeStruct((T, D), BF16)] * 2,
        scratch_shapes=[pltpu.VMEM((tt + HALO_A, D), F32), pltpu.VMEM((tt + HALO_B, D), F32),
                        pltpu.VMEM((tt, D), F32), pltpu.VMEM((7, CONV_RC + HALO_A, CONV_CC), F32)],
        compiler_params=_params("arbitrary"),
        name="conv_prompt",
    )(glu, glu, cgh, cgh, bg, wa, ba, lg, lb, wb)


def _conv_sample_kernel(g_ref, sa_ref, c_ref, sb_ref, bg_ref, wa_ref, ba_ref, lg_ref, lb_ref,
                        wb_ref, ya_ref, yb_ref, pre_ref, *, n_pos, bs):
    n_hist_a = CONF_W - 1
    n_hist_b = SC_W - 1

    def chunk(c, carry):
        cs = pl.ds(pl.multiple_of(c * CONV_CC, CONV_CC), CONV_CC)
        wa = wa_ref[:, cs]
        wb = wb_ref[:, cs]
        bias = jnp.broadcast_to(ba_ref[:, cs], (bs, CONV_CC))
        acc = [bias] * n_pos
        for j in range(n_hist_a + n_pos):
            row = sa_ref[j, :, cs] if j < n_hist_a else g_ref[j - n_hist_a, :, cs]
            for p in range(n_pos):
                k = j - p
                if 0 <= k < CONF_W:
                    acc[p] = acc[p] + row * wa[k:k + 1, :]
        for p in range(n_pos):
            pre_ref[p, :, cs] = acc[p]
        rows_b = [sb_ref[j, :, cs] for j in range(n_hist_b)] + [c_ref[p, :, cs] for p in range(n_pos)]
        for p in range(n_pos):
            accb = rows_b[p] * wb[0:1, :]
            for k in range(1, SC_W):
                accb = accb + rows_b[p + k] * wb[k:k + 1, :]
            yb_ref[p, :, cs] = (bg_ref[p, :, cs] * accb).astype(BF16)
        return carry

    lax.fori_loop(0, D // CONV_CC, chunk, 0)
    for p in range(n_pos):
        y = _layernorm(pre_ref[p], lg_ref[...], lb_ref[...])
        ya_ref[p] = (y * jax.nn.sigmoid(y)).astype(BF16)


def _conv_sample(glu, cgh, bg, state_a, state_b, wa, ba, lg, lb, wb, bs=32):
    n_pos, n_seq, _ = glu.shape
    slab = pl.BlockSpec((n_pos, bs, D), lambda i: (0, i, 0))
    const = lambda shape: pl.BlockSpec(shape, lambda i: (0, 0))
    return pl.pallas_call(
        functools.partial(_conv_sample_kernel, n_pos=n_pos, bs=bs),
        grid=(n_seq // bs,),
        in_specs=[slab, pl.BlockSpec((CONF_W - 1, bs, D), lambda i: (0, i, 0)),
                  slab, pl.BlockSpec((SC_W - 1, bs, D), lambda i: (0, i, 0)),
                  slab, const((32, D)), const((1, D)), const((1, D)), const((1, D)), const((8, D))],
        out_specs=[slab, slab],
        out_shape=[jax.ShapeDtypeStruct((n_pos, n_seq, D), BF16)] * 2,
        scratch_shapes=[pltpu.VMEM((n_pos, bs, D), F32)],
        compiler_params=_params("arbitrary"),
        name="conv_sample",
    )(glu, state_a, cgh, state_b, bg, wa, ba, lg, lb, wb)


def _branch_kernel(ya_ref, yb_ref, sga_ref, sgb_ref, wa_ref, wb_ref, mm_ref):
    ya = jnp.dot(ya_ref[...], wa_ref[...], preferred_element_type=F32)
    yb = jnp.dot(yb_ref[...], wb_ref[...], preferred_element_type=F32)
    mm_ref[...] = (sga_ref[...] * ya + sgb_ref[...] * yb).astype(BF16)


def _ln1_kernel(mm_ref, x_ref, g1_ref, sc2_ref, sh2_ref, wo_ref, lg_ref, lb_ref, x1_ref, u2_ref,
                *, alpha):
    m = jnp.dot(mm_ref[...], wo_ref[...], preferred_element_type=F32)
    h = alpha * x_ref[...] + (1.0 + g1_ref[...]) * m
    x1 = _layernorm(h, lg_ref[...], lb_ref[...])
    x1_ref[...] = x1
    u2_ref[...] = (x1 * (1.0 + sc2_ref[...]) + sh2_ref[...]).astype(BF16)


def _out_proj(ya_in, yb_in, sga, sgb, x2d, mod, per_token, tiles_per_seq, w_a, w_b, w_o,
              lg, lb, alpha, tt=512, tn=512):
    T = x2d.shape[0]
    row2 = pl.BlockSpec((tt, D), lambda i, j: (i, 0))
    col2 = pl.BlockSpec((tt, tn), lambda i, j: (i, j))
    wcol = pl.BlockSpec((D, tn), lambda i, j: (0, j))
    mm = pl.pallas_call(
        _branch_kernel,
        grid=(T // tt, D // tn),
        in_specs=[row2, row2, col2, col2, wcol, wcol],
        out_specs=col2,
        out_shape=jax.ShapeDtypeStruct((T, D), BF16),
        compiler_params=_params("arbitrary", "arbitrary"),
        name="branch_proj",
    )(ya_in, yb_in, sga, sgb, w_a, w_b)

    spec = _mod_spec_fn(per_token, tt, tiles_per_seq)
    row = pl.BlockSpec((tt, D), lambda i: (i, 0))
    const = pl.BlockSpec((1, D), lambda i: (0, 0))
    return pl.pallas_call(
        functools.partial(_ln1_kernel, alpha=alpha),
        grid=(T // tt,),
        in_specs=[row, row, spec(2), spec(4), spec(3), pl.BlockSpec((D, D), lambda i: (0, 0)),
                  const, const],
        out_specs=[row, row],
        out_shape=[jax.ShapeDtypeStruct((T, D), F32), jax.ShapeDtypeStruct((T, D), BF16)],
        compiler_params=_params("arbitrary"),
        name="out_proj_ln1",
    )(mm, x2d, mod, mod, mod, w_o, lg, lb)


def _score_kernel(u2_ref, wq_ref, keys_ref, st_ref):
    q = jnp.dot(u2_ref[...], wq_ref[...], preferred_element_type=F32).astype(BF16)
    for c in range(2 * N_HEADS):
        st_ref[c * N_KEYS:(c + 1) * N_KEYS, :] = lax.dot_general(
            keys_ref[c], q[:, c * N_KEYS:(c + 1) * N_KEYS], (((1,), (1,)), ((), ())),
            preferred_element_type=F32)


def _scores(u2, w_q, keys, tt=512):
    T = u2.shape[0]
    return pl.pallas_call(
        _score_kernel,
        grid=(T // tt,),
        in_specs=[pl.BlockSpec((tt, D), lambda i: (i, 0)),
                  pl.BlockSpec((D, D), lambda i: (0, 0)),
                  pl.BlockSpec((2 * N_HEADS, N_KEYS, N_KEYS), lambda i: (0, 0, 0))],
        out_specs=pl.BlockSpec((D, tt), lambda i: (0, i)),
        out_shape=jax.ShapeDtypeStruct((D, T), F32),
        compiler_params=_params("arbitrary"),
        name="peer_scores",
    )(u2, w_q, keys)


TOPK_SUB = 256


def _sort16_network():
    def merge(lo, hi, r):
        step = r * 2
        if step < hi - lo:
            yield from merge(lo, hi, step)
            yield from merge(lo + r, hi, step)
            yield from ((i, i + r) for i in range(lo + r, hi - r, step))
        else:
            yield (lo, lo + r)

    def sort(lo, hi):
        if hi > lo:
            mid = lo + (hi - lo) // 2
            yield from sort(lo, mid)
            yield from sort(mid + 1, hi)
            yield from merge(lo, hi, 1)

    return tuple(sort(0, TOPK - 1))


_SORT16 = _sort16_network()
_BITONIC16 = tuple((i, i + d) for d in (8, 4, 2, 1) for i in range(TOPK) if not i & d)


def _compare_exchange(v, net):
    for a, b in net:
        v[a], v[b] = jnp.maximum(v[a], v[b]), jnp.minimum(v[a], v[b])
    return v


def _top_values(s):
    assert s.shape[0] == 8 * TOPK
    v = _compare_exchange([s[8 * m:8 * m + 8, :] for m in range(TOPK)], _SORT16)
    for sh in (4, 2, 1):
        rot = [pltpu.roll(x, 8 - sh, axis=0) for x in v]
        v = _compare_exchange([jnp.maximum(v[i], rot[TOPK - 1 - i]) for i in range(TOPK)], _BITONIC16)
    return [x[0:1, :] for x in v]


def _topk_kernel(st_ref, e1_ref, e2_ref, phi_ref, v1_ref, v2_ref, ev2_ref, aux_ref, *, tl):
    n_grp = tl // LANES
    big = -NEG_BIG
    c_any = TOPK // 2

    def head(h, carry):
        r1 = pl.multiple_of(h * 2 * N_KEYS, 2 * N_KEYS)
        r2 = r1 + N_KEYS
        for q in range(tl // TOPK_SUB):
            lo = q * TOPK_SUB
            for half, vref in ((r1, v1_ref), (r2, v2_ref)):
                vals = _top_values(st_ref[pl.ds(half, N_KEYS), lo:lo + TOPK_SUB])
                for r in range(TOPK):
                    for g in range(TOPK_SUB // LANES):
                        gg = q * (TOPK_SUB // LANES) + g
                        vref[r, gg:gg + 1, :] = vals[r][:, g * LANES:(g + 1) * LANES]
        v1 = [v1_ref[r] for r in range(TOPK)]
        v2 = [v2_ref[r] for r in range(TOPK)]
        cands = [v1[r] + v2[c] for (r, c) in _CAND]
        m = functools.reduce(jnp.maximum, cands)
        for _ in range(TOPK - 1):
            m = functools.reduce(jnp.maximum, [jnp.where(cd < m, cd, NEG_BIG) for cd in cands])
        tau = m
        top = cands[0]
        z = functools.reduce(
            lambda a, b: a + b, [jnp.where(cd >= tau, jnp.exp(cd - top), 0.0) for cd in cands])
        ev2 = [jnp.exp(v2[c] - v2[0]) for c in range(TOPK)]
        for c in range(TOPK):
            ev2_ref[c] = ev2[c]
        phi_top = functools.reduce(
            jnp.minimum, [jnp.where(v1[0] + v2[c] >= tau, ev2[c], big) for c in range(c_any, TOPK)])
        aux_ref[0] = tau
        aux_ref[1] = 1.0 / z
        aux_ref[2] = phi_top
        for g in range(n_grp):
            ls = slice(g * LANES, (g + 1) * LANES)
            row = lambda ref, r: ref[r, g:g + 1, :]
            s1 = st_ref[pl.ds(r1, N_KEYS), ls]
            s2 = st_ref[pl.ds(r2, N_KEYS), ls]
            tau_row = row(aux_ref, 0)
            phi = jnp.where(s1 + row(v2_ref, 0) >= tau_row, row(ev2_ref, 0), big)
            for c in range(1, c_any):
                phi = jnp.minimum(phi, jnp.where(s1 + row(v2_ref, c) >= tau_row, row(ev2_ref, c), big))
            phi = jnp.where(s1 >= row(v1_ref, 0), jnp.minimum(phi, row(aux_ref, 2)), phi)
            e1 = jnp.where(s1 >= row(v1_ref, TOPK - 1), jnp.exp(s1 - row(v1_ref, 0)), 0.0)
            e2 = jnp.where(s2 >= row(v2_ref, TOPK - 1), jnp.exp(s2 - row(v2_ref, 0)), 0.0)
            e1_ref[h, :, ls] = e1 * row(aux_ref, 1)
            e2_ref[h, :, ls] = e2
            phi_ref[h, :, ls] = phi
        return carry

    lax.fori_loop(0, N_HEADS, head, 0)


def _topk(st, tl=512):
    T = st.shape[1]
    n_grp = tl // LANES
    e_spec = pl.BlockSpec((N_HEADS, N_KEYS, tl), lambda i: (0, 0, i))
    return pl.pallas_call(
        functools.partial(_topk_kernel, tl=tl),
        grid=(T // tl,),
        in_specs=[pl.BlockSpec((D, tl), lambda i: (0, i))],
        out_specs=[e_spec, e_spec, e_spec],
        out_shape=[jax.ShapeDtypeStruct((N_HEADS, N_KEYS, T), F32)] * 3,
        scratch_shapes=[pltpu.VMEM((TOPK, n_grp, LANES), F32), pltpu.VMEM((TOPK, n_grp, LANES), F32),
                        pltpu.VMEM((TOPK, n_grp, LANES), F32), pltpu.VMEM((3, n_grp, LANES), F32)],
        compiler_params=_params("arbitrary"),
        name="peer_topk",
    )(st)


PEER_TC = 256


PEER_IROWS = 8


def _peer_gate_tile(row, col, act_ref, e1_ref, phi_ref, e2_ref, g_ref, *, tt):
    cols = slice(col * N_KEYS, (col + 1) * N_KEYS)
    for t0 in range(0, tt, PEER_TC):
        ts = slice(t0, t0 + PEER_TC)
        w = jnp.zeros((N_KEYS, PEER_TC), F32)
        for h in range(N_HEADS):
            e1row = e1_ref[h, row:row + 1, ts]
            phirow = phi_ref[h, row:row + 1, ts]
            e2 = e2_ref[h, :, ts]
            w = w + jnp.where(e2 >= phirow, e2, 0.0) * e1row
        g_ref[ts, cols] = (w.T * act_ref[ts, cols]).astype(BF16)


def _peer_kernel(u2_ref, pu_ref, pun_ref, pv_ref, pvb_ref, e1_ref, phi_ref, e2_ref, o_ref,
                 acta_ref, actb_ref, ga_ref, gb_ref, ob_ref, *, tt, te, n_k):
    k = pl.program_id(1)
    half = te // 2
    n_il = half // N_KEYS
    sub = half // 2
    wide = D // 2

    def act_of(rows):
        a = lax.dot_general(u2_ref[...], rows, (((1,), (1,)), ((), ())), preferred_element_type=F32)
        return 0.5 * a * (1.0 + lax.erf(a * INV_SQRT2))

    @pl.when(k == 0)
    def _():
        o_ref[...] = jnp.zeros((tt, D), F32)
        ob_ref[...] = jnp.zeros((tt, D), F32)
        gb_ref[...] = jnp.zeros((tt, half), BF16)
        acta_ref[...] = act_of(pu_ref[0:half, :])

    def act_job(dst_ref, src_ref, r0, j):
        def run():
            dst_ref[:, j * sub:(j + 1) * sub] = act_of(src_ref[r0 + j * sub:r0 + (j + 1) * sub, :])
        return run

    def mix_job(acc_ref, g_ref, v_ref, j):
        def run():
            cs = slice(j * wide, (j + 1) * wide)
            acc_ref[:, cs] += jnp.dot(g_ref[...], v_ref[0:half, cs], preferred_element_type=F32)
        return run

    jobs_a = [act_job(actb_ref, pu_ref, half, 0), mix_job(ob_ref, gb_ref, pvb_ref, 0),
              act_job(actb_ref, pu_ref, half, 1), mix_job(ob_ref, gb_ref, pvb_ref, 1)]
    jobs_b = [act_job(acta_ref, pun_ref, 0, 0), mix_job(o_ref, ga_ref, pv_ref, 0),
              act_job(acta_ref, pun_ref, 0, 1), mix_job(o_ref, ga_ref, pv_ref, 1)]
    gate = functools.partial(_peer_gate_tile, e1_ref=e1_ref, phi_ref=phi_ref, e2_ref=e2_ref, tt=tt)
    for il in range(n_il):
        jobs_a[il]()
        gate(il, il, acta_ref, g_ref=ga_ref)
    for il in range(n_il):
        jobs_b[il]()
        gate(n_il + il, il, actb_ref, g_ref=gb_ref)

    @pl.when(k == n_k - 1)
    def _():
        o_ref[...] += ob_ref[...] + jnp.dot(gb_ref[...], pv_ref[half:te, :],
                                            preferred_element_type=F32)


def _peer(u2, peer_u, peer_v, e1, phi, e2, tt=512, te=1024):
    T = u2.shape[0]
    n_exp = peer_u.shape[0]
    assert te == PEER_IROWS * N_KEYS and n_exp % te == 0
    n_k = n_exp // te
    half = te // 2
    e1r = e1.reshape(N_HEADS, n_k, PEER_IROWS, T)
    phir = phi.reshape(N_HEADS, n_k, PEER_IROWS, T)
    rows_spec = pl.BlockSpec((N_HEADS, None, PEER_IROWS, tt), lambda i, k: (0, k, 0, i))
    tile = pl.BlockSpec((te, D), lambda i, k: (k, 0))
    return pl.pallas_call(
        functools.partial(_peer_kernel, tt=tt, te=te, n_k=n_k),
        grid=(T // tt, n_k),
        in_specs=[pl.BlockSpec((tt, D), lambda i, k: (i, 0)),
                  tile,
                  pl.BlockSpec((half, D), lambda i, k: (jnp.minimum(2 * k + 2, 2 * n_k - 2), 0)),
                  tile,
                  pl.BlockSpec((half, D), lambda i, k: (jnp.maximum(2 * k - 1, 0), 0)),
                  rows_spec, rows_spec,
                  pl.BlockSpec((N_HEADS, N_KEYS, tt), lambda i, k: (0, 0, i))],
        out_specs=pl.BlockSpec((tt, D), lambda i, k: (i, 0)),
        out_shape=jax.ShapeDtypeStruct((T, D), F32),
        scratch_shapes=[pltpu.VMEM((tt, half), F32), pltpu.VMEM((tt, half), F32),
                        pltpu.VMEM((tt, half), BF16), pltpu.VMEM((tt, half), BF16),
                        pltpu.VMEM((tt, D), F32)],
        compiler_params=_params("arbitrary", "arbitrary"),
        name="peer_mix",
    )(u2, peer_u, peer_u, peer_v, peer_v, e1r, phir, e2)


def _final_kernel(f_ref, x1_ref, g2_ref, lg_ref, lb_ref, y_ref, *, alpha):
    h = alpha * x1_ref[...] + (1.0 + g2_ref[...]) * f_ref[...]
    y_ref[...] = _layernorm(h, lg_ref[...], lb_ref[...])


def _final(f, row0, x1, mod, per_token, tiles_per_seq, lg, lb, alpha, tt=256):
    T = x1.shape[0]
    spec = _mod_spec_fn(per_token, tt, tiles_per_seq)
    off = row0 // tt
    row = pl.BlockSpec((tt, D), lambda i: (i, 0))
    const = pl.BlockSpec((1, D), lambda i: (0, 0))
    return pl.pallas_call(
        functools.partial(_final_kernel, alpha=alpha),
        grid=(T // tt,),
        in_specs=[pl.BlockSpec((tt, D), lambda i: (off + i, 0)), row, spec(5), const, const],
        out_specs=row,
        out_shape=jax.ShapeDtypeStruct((T, D), F32),
        compiler_params=_params("arbitrary"),
        name="final_ln",
    )(f, x1, mod, lg, lb)


def _pad_rows(w, n):
    return jnp.pad(w, ((0, n - w.shape[0]), (0, 0)))


def _layer(xp, xs, c_all, n_p, state_a, state_b, w_mod, b_mod, w_in, conv_a_w, conv_a_b, ln_a_g,
           ln_a_b, w_a_out, conv_b_w, w_b_out, w_o, ln1_g, ln1_b, w_q, sub_keys, peer_u, peer_v,
           ln2_g, ln2_b, alpha, seq_len, n_pos):
    n_s = xs.shape[0] // n_pos
    Tp = xp.shape[0]
    row = lambda v: v.reshape(1, D)

    mod = _modulation(c_all, w_mod, b_mod)
    mod_p = mod[:n_p].reshape(n_p, 6, 1, D)
    mod_s = jnp.tile(mod[n_p:n_p + n_s].reshape(n_s, 6, D).transpose(1, 0, 2), (1, n_pos, 1))

    w_in_b = w_in.astype(BF16)
    wa = _pad_rows(conv_a_w, 32)
    wb = _pad_rows(conv_b_w, 8)
    conv_args = (wa, row(conv_a_b), row(ln_a_g), row(ln_a_b), wb)

    glu_p, cgh_p, bg_p, sga_p, sgb_p = _in_proj(xp, mod_p, False, seq_len // 512, w_in_b)
    ya_p, yb_p = _conv_prompt(glu_p, cgh_p, bg_p, *conv_args, seq_len=seq_len)
    glu_s, cgh_s, bg_s, sga_s, sgb_s = _in_proj(xs, mod_s, True, 1, w_in_b)
    slab = lambda v: v.reshape(n_pos, n_s, D)
    ya_s, yb_s = _conv_sample(slab(glu_s), slab(cgh_s), slab(bg_s), state_a.transpose(1, 0, 2),
                              state_b.transpose(1, 0, 2), *conv_args)

    proj_w = (w_a_out.astype(BF16), w_b_out.astype(BF16), w_o.astype(BF16), row(ln1_g), row(ln1_b))
    x1_p, u2_p = _out_proj(ya_p, yb_p, sga_p, sgb_p, xp, mod_p, False, seq_len // 512, *proj_w, alpha)
    x1_s, u2_s = _out_proj(ya_s.reshape(-1, D), yb_s.reshape(-1, D), sga_s, sgb_s, xs, mod_s, True, 1,
                           *proj_w, alpha)

    u2 = jnp.concatenate([u2_p, u2_s], axis=0)
    keys = sub_keys.astype(BF16).reshape(2 * N_HEADS, N_KEYS, N_KEYS)
    st = _scores(u2, w_q.astype(BF16), keys)
    e1, e2, phi = _topk(st)
    f = _peer(u2, peer_u.astype(BF16), peer_v.astype(BF16), e1, phi, e2)

    y_p = _final(f, 0, x1_p, mod_p, False, seq_len // 256, row(ln2_g), row(ln2_b), alpha)
    y_s = _final(f, Tp, x1_s, mod_s, True, 1, row(ln2_g), row(ln2_b), alpha)

    new_a_p = glu_p.reshape(n_p, seq_len, D)[:, seq_len - (CONF_W - 1):]
    new_b_p = cgh_p.reshape(n_p, seq_len, D)[:, seq_len - (SC_W - 1):]
    glu_s_sm = slab(glu_s).transpose(1, 0, 2)
    cgh_s_sm = slab(cgh_s).transpose(1, 0, 2)
    new_a_s = jnp.concatenate([state_a, glu_s_sm], axis=1)[:, -(CONF_W - 1):]
    new_b_s = jnp.concatenate([state_b, cgh_s_sm], axis=1)[:, -(SC_W - 1):]
    return y_p, y_s, new_a_p, new_b_p, new_a_s, new_b_s


def kernel(x_prompt, x_sample, state_conv_a, state_conv_b, c_prompt, c_sample, w_mod, b_mod, w_in, conv_a_w, conv_a_b, ln_a_g, ln_a_b, w_a_out, conv_b_w, w_b_out, w_o, ln1_g, ln1_b, w_q, sub_keys, peer_u, peer_v, ln2_g, ln2_b):
    n_p, seq_len, d = x_prompt.shape
    n_s, n_pos, _ = x_sample.shape
    depth = w_mod.shape[0]
    assert d == D and seq_len % 512 == 0 and (n_s * n_pos) % 512 == 0 and n_s % 32 == 0
    alpha = (2.0 * depth) ** 0.25

    c_all = jnp.concatenate([c_prompt, c_sample], axis=0)
    c_all = _pad_rows(c_all, -(-c_all.shape[0] // 8) * 8)
    xp = x_prompt.reshape(n_p * seq_len, D)
    xs = x_sample.transpose(1, 0, 2).reshape(n_pos * n_s, D)

    layer_w = (w_mod, b_mod, w_in, conv_a_w, conv_a_b, ln_a_g, ln_a_b, w_a_out, conv_b_w, w_b_out,
               w_o, ln1_g, ln1_b, w_q, sub_keys, peer_u, peer_v, ln2_g, ln2_b)
    na_p, nb_p, na_s, nb_s = [], [], [], []
    for l in range(depth):
        xp, xs, a_p, b_p, a_s, b_s = _layer(
            xp, xs, c_all, n_p, state_conv_a[l], state_conv_b[l], *[w[l] for w in layer_w],
            alpha=alpha, seq_len=seq_len, n_pos=n_pos)
        na_p.append(a_p)
        nb_p.append(b_p)
        na_s.append(a_s)
        nb_s.append(b_s)
    y_prompt = xp.reshape(n_p, seq_len, D)
    y_sample = xs.reshape(n_pos, n_s, D).transpose(1, 0, 2)
    return (y_prompt, y_sample, jnp.stack(na_p), jnp.stack(nb_p), jnp.stack(na_s), jnp.stack(nb_s))
```

```python
import functools

import jax
import jax.numpy as jnp
from jax import lax
from jax.experimental import pallas as pl
from jax.experimental.pallas import tpu as pltpu

F32 = jnp.float32
BF16 = jnp.bfloat16

D = 2048
CONF_W = 31
SC_W = 3
N_HEADS = 8
N_KEYS = 128
TOPK = 16
LN_EPS = 1e-5
INV_SQRT2 = 0.7071067811865476
NEG_BIG = -3.0e38

LANES = 128
VMEM_LIMIT = 56 * 1024 * 1024

_CAND = tuple((r, c) for r in range(TOPK) for c in range(TOPK) if (r + 1) * (c + 1) <= TOPK)


def _params(*sem, flags=None):
    return pltpu.CompilerParams(dimension_semantics=sem, vmem_limit_bytes=VMEM_LIMIT, flags=flags)


def _layernorm(x, g, b):
    mu = jnp.mean(x, axis=-1, keepdims=True)
    xc = x - mu
    var = jnp.mean(xc * xc, axis=-1, keepdims=True)
    return xc * lax.rsqrt(var + LN_EPS) * g + b


def _mod_spec_fn(per_token, tt, tiles_per_seq):
    def spec(k):
        if per_token:
            return pl.BlockSpec((None, tt, D), lambda i, *_: (k, i, 0))
        return pl.BlockSpec((None, None, 1, D), lambda i, *_: (i // tiles_per_seq, k, 0, 0))
    return spec


def _mod_kernel(c_ref, w_ref, b_ref, o_ref):
    c = c_ref[...]
    s = (c * jax.nn.sigmoid(c)).astype(BF16)
    o_ref[...] = jnp.dot(s, w_ref[...].astype(BF16), preferred_element_type=F32) + b_ref[...]


def _modulation(c_all, w_mod, b_mod):
    n = c_all.shape[0]
    tn = 1024
    return pl.pallas_call(
        _mod_kernel,
        grid=(6 * D // tn,),
        in_specs=[pl.BlockSpec((n, D), lambda j: (0, 0)),
                  pl.BlockSpec((D, tn), lambda j: (0, j)),
                  pl.BlockSpec((1, tn), lambda j: (0, j))],
        out_specs=pl.BlockSpec((n, tn), lambda j: (0, j)),
        out_shape=jax.ShapeDtypeStruct((n, 6 * D), F32),
        compiler_params=_params("arbitrary"),
        name="modulation",
    )(c_all, w_mod, b_mod.reshape(1, 6 * D))


def _inproj_kernel(x_ref, sc_ref, sh_ref, w0, w1, w2, w3, w4, w5, w6,
                   glu_ref, cgh_ref, bg_ref, sga_ref, sgb_ref):
    u = (x_ref[...] * (1.0 + sc_ref[...]) + sh_ref[...]).astype(BF16)

    def z(w):
        return jnp.dot(u, w[...], preferred_element_type=F32)

    glu_ref[...] = z(w0) * jax.nn.sigmoid(z(w1))
    bg_ref[...] = z(w2)
    cgh_ref[...] = z(w3) * z(w4)
    sga_ref[...] = jax.nn.sigmoid(z(w5))
    sgb_ref[...] = jax.nn.sigmoid(z(w6))


def _in_proj(x2d, mod, per_token, tiles_per_seq, w_in, tt=512, tn=512):
    T = x2d.shape[0]
    spec = _mod_spec_fn(per_token, tt, tiles_per_seq)
    nj = D // tn
    w_specs = [pl.BlockSpec((D, tn), lambda i, j, g=g: (0, g * nj + j)) for g in range(7)]
    o_spec = pl.BlockSpec((tt, tn), lambda i, j: (i, j))
    return pl.pallas_call(
        _inproj_kernel,
        grid=(T // tt, nj),
        in_specs=[pl.BlockSpec((tt, D), lambda i, j: (i, 0)), spec(1), spec(0)] + w_specs,
        out_specs=[o_spec] * 5,
        out_shape=[jax.ShapeDtypeStruct((T, D), F32)] * 5,
        compiler_params=_params("arbitrary", "arbitrary"),
        name="in_proj",
    )(x2d, mod, mod, *([w_in] * 7))


CONV_CC = 256
CONV_RC = 128
HALO_A = 32
HALO_B = 8


def _conv_prompt_kernel(g_ref, gh_ref, c_ref, ch_ref, bg_ref, wa_ref, ba_ref, lg_ref, lb_ref,
                        wb_ref, ya_ref, yb_ref, ext_ref, extb_ref, pre_ref, sh_ref, *, tt, tiles_per_seq):
    i = pl.program_id(0)
    first = i % tiles_per_seq == 0

    @pl.when(first)
    def _():
        ext_ref[0:HALO_A, :] = jnp.zeros((HALO_A, D), F32)
        extb_ref[0:HALO_B, :] = jnp.zeros((HALO_B, D), F32)

    @pl.when(jnp.logical_not(first))
    def _():
        ext_ref[0:HALO_A, :] = gh_ref[...]
        extb_ref[0:HALO_B, :] = ch_ref[...]

    ext_ref[HALO_A:, :] = g_ref[...]
    extb_ref[HALO_B:, :] = c_ref[...]

    def chunk(c, carry):
        cs = pl.ds(pl.multiple_of(c * CONV_CC, CONV_CC), CONV_CC)
        wa = wa_ref[:, cs]
        wb = wb_ref[:, cs]
        bias = jnp.broadcast_to(ba_ref[:, cs], (CONV_RC, CONV_CC))
        for r in range(tt // CONV_RC):
            r0 = r * CONV_RC
            acc = bias
            lead = HALO_A - (CONF_W - 1)
            for b in range(8):
                taps = [k for k in range(CONF_W) if (lead + k) % 8 == b]
                span = (lead + taps[-1]) // 8 * 8 + CONV_RC
                if b:
                    sh_ref[b - 1, 0:span, :] = ext_ref[pl.ds(r0 + b, span), cs]
                for k in taps:
                    a8 = (lead + k) // 8 * 8
                    win = (sh_ref[b - 1, a8:a8 + CONV_RC, :] if b
                           else ext_ref[pl.ds(r0 + a8, CONV_RC), cs])
                    acc = acc + win * wa[k:k + 1, :]
            pre_ref[pl.ds(r0, CONV_RC), cs] = acc
            accb = extb_ref[pl.ds(r0 + HALO_B - (SC_W - 1), CONV_RC), cs] * wb[0:1, :]
            for k in range(1, SC_W):
                accb = accb + extb_ref[pl.ds(r0 + HALO_B - (SC_W - 1) + k, CONV_RC), cs] * wb[k:k + 1, :]
            yb_ref[pl.ds(r0, CONV_RC), cs] = (bg_ref[pl.ds(r0, CONV_RC), cs] * accb).astype(BF16)
        return carry

    lax.fori_loop(0, D // CONV_CC, chunk, 0)
    y = _layernorm(pre_ref[...], lg_ref[...], lb_ref[...])
    ya_ref[...] = (y * jax.nn.sigmoid(y)).astype(BF16)


def _conv_prompt(glu, cgh, bg, wa, ba, lg, lb, wb, seq_len, tt=256):
    T = glu.shape[0]
    tiles_per_seq = seq_len // tt
    row = pl.BlockSpec((tt, D), lambda i: (i, 0))
    const = lambda shape: pl.BlockSpec(shape, lambda i: (0, 0))
    halo_a = pl.BlockSpec((HALO_A, D), lambda i: (jnp.maximum(i * (tt // HALO_A) - 1, 0), 0))
    halo_b = pl.BlockSpec((HALO_B, D), lambda i: (jnp.maximum(i * (tt // HALO_B) - 1, 0), 0))
    return pl.pallas_call(
        functools.partial(_conv_prompt_kernel, tt=tt, tiles_per_seq=tiles_per_seq),
        grid=(T // tt,),
        in_specs=[row, halo_a, row, halo_b, row, const((32, D)), const((1, D)), const((1, D)),
                  const((1, D)), const((8, D))],
        out_specs=[row, row],
        out_shape=[jax.ShapeDtypeStruct((T, D), BF16)] * 2,
        scratch_shapes=[pltpu.VMEM((tt + HALO_A, D), F32), pltpu.VMEM((tt + HALO_B, D), F32),
                        pltpu.VMEM((tt, D), F32), pltpu.VMEM((7, CONV_RC + HALO_A, CONV_CC), F32)],
        compiler_params=_params("arbitrary"),
        name="conv_prompt",
    )(glu, glu, cgh, cgh, bg, wa, ba, lg, lb, wb)


def _conv_sample_kernel(g_ref, sa_ref, c_ref, sb_ref, bg_ref, wa_ref, ba_ref, lg_ref, lb_ref,
                        wb_ref, ya_ref, yb_ref, pre_ref, *, n_pos, bs):
    n_hist_a = CONF_W - 1
    n_hist_b = SC_W - 1

    def chunk(c, carry):
        cs = pl.ds(pl.multiple_of(c * CONV_CC, CONV_CC), CONV_CC)
        wa = wa_ref[:, cs]
        wb = wb_ref[:, cs]
        bias = jnp.broadcast_to(ba_ref[:, cs], (bs, CONV_CC))
        acc = [bias] * n_pos
        for j in range(n_hist_a + n_pos):
            row = sa_ref[j, :, cs] if j < n_hist_a else g_ref[j - n_hist_a, :, cs]
            for p in range(n_pos):
                k = j - p
                if 0 <= k < CONF_W:
                    acc[p] = acc[p] + row * wa[k:k + 1, :]
        for p in range(n_pos):
            pre_ref[p, :, cs] = acc[p]
        rows_b = [sb_ref[j, :, cs] for j in range(n_hist_b)] + [c_ref[p, :, cs] for p in range(n_pos)]
        for p in range(n_pos):
            accb = rows_b[p] * wb[0:1, :]
            for k in range(1, SC_W):
                accb = accb + rows_b[p + k] * wb[k:k + 1, :]
            yb_ref[p, :, cs] = (bg_ref[p, :, cs] * accb).astype(BF16)
        return carry

    lax.fori_loop(0, D // CONV_CC, chunk, 0)
    for p in range(n_pos):
        y = _layernorm(pre_ref[p], lg_ref[...], lb_ref[...])
        ya_ref[p] = (y * jax.nn.sigmoid(y)).astype(BF16)


def _conv_sample(glu, cgh, bg, state_a, state_b, wa, ba, lg, lb, wb, bs=32):
    n_pos, n_seq, _ = glu.shape
    slab = pl.BlockSpec((n_pos, bs, D), lambda i: (0, i, 0))
    const = lambda shape: pl.BlockSpec(shape, lambda i: (0, 0))
    return pl.pallas_call(
        functools.partial(_conv_sample_kernel, n_pos=n_pos, bs=bs),
        grid=(n_seq // bs,),
        in_specs=[slab, pl.BlockSpec((CONF_W - 1, bs, D), lambda i: (0, i, 0)),
                  slab, pl.BlockSpec((SC_W - 1, bs, D), lambda i: (0, i, 0)),
                  slab, const((32, D)), const((1, D)), const((1, D)), const((1, D)), const((8, D))],
        out_specs=[slab, slab],
        out_shape=[jax.ShapeDtypeStruct((n_pos, n_seq, D), BF16)] * 2,
        scratch_shapes=[pltpu.VMEM((n_pos, bs, D), F32)],
        compiler_params=_params("arbitrary"),
        name="conv_sample",
    )(glu, state_a, cgh, state_b, bg, wa, ba, lg, lb, wb)


def _branch_kernel(ya_ref, yb_ref, sga_ref, sgb_ref, wa_ref, wb_ref, mm_ref):
    ya = jnp.dot(ya_ref[...], wa_ref[...], preferred_element_type=F32)
    yb = jnp.dot(yb_ref[...], wb_ref[...], preferred_element_type=F32)
    mm_ref[...] = (sga_ref[...] * ya + sgb_ref[...] * yb).astype(BF16)


def _ln1_kernel(mm_ref, x_ref, g1_ref, sc2_ref, sh2_ref, wo_ref, lg_ref, lb_ref, x1_ref, u2_ref,
                *, alpha):
    m = jnp.dot(mm_ref[...], wo_ref[...], preferred_element_type=F32)
    h = alpha * x_ref[...] + (1.0 + g1_ref[...]) * m
    x1 = _layernorm(h, lg_ref[...], lb_ref[...])
    x1_ref[...] = x1
    u2_ref[...] = (x1 * (1.0 + sc2_ref[...]) + sh2_ref[...]).astype(BF16)


def _out_proj(ya_in, yb_in, sga, sgb, x2d, mod, per_token, tiles_per_seq, w_a, w_b, w_o,
              lg, lb, alpha, tt=512, tn=512):
    T = x2d.shape[0]
    row2 = pl.BlockSpec((tt, D), lambda i, j: (i, 0))
    col2 = pl.BlockSpec((tt, tn), lambda i, j: (i, j))
    wcol = pl.BlockSpec((D, tn), lambda i, j: (0, j))
    mm = pl.pallas_call(
        _branch_kernel,
        grid=(T // tt, D // tn),
        in_specs=[row2, row2, col2, col2, wcol, wcol],
        out_specs=col2,
        out_shape=jax.ShapeDtypeStruct((T, D), BF16),
        compiler_params=_params("arbitrary", "arbitrary"),
        name="branch_proj",
    )(ya_in, yb_in, sga, sgb, w_a, w_b)

    spec = _mod_spec_fn(per_token, tt, tiles_per_seq)
    row = pl.BlockSpec((tt, D), lambda i: (i, 0))
    const = pl.BlockSpec((1, D), lambda i: (0, 0))
    return pl.pallas_call(
        functools.partial(_ln1_kernel, alpha=alpha),
        grid=(T // tt,),
        in_specs=[row, row, spec(2), spec(4), spec(3), pl.BlockSpec((D, D), lambda i: (0, 0)),
                  const, const],
        out_specs=[row, row],
        out_shape=[jax.ShapeDtypeStruct((T, D), F32), jax.ShapeDtypeStruct((T, D), BF16)],
        compiler_params=_params("arbitrary"),
        name="out_proj_ln1",
    )(mm, x2d, mod, mod, mod, w_o, lg, lb)


def _score_kernel(u2_ref, wq_ref, keys_ref, st_ref):
    q = jnp.dot(u2_ref[...], wq_ref[...], preferred_element_type=F32).astype(BF16)
    for c in range(2 * N_HEADS):
        st_ref[c * N_KEYS:(c + 1) * N_KEYS, :] = lax.dot_general(
            keys_ref[c], q[:, c * N_KEYS:(c + 1) * N_KEYS], (((1,), (1,)), ((), ())),
            preferred_element_type=F32)


def _scores(u2, w_q, keys, tt=512):
    T = u2.shape[0]
    return pl.pallas_call(
        _score_kernel,
        grid=(T // tt,),
        in_specs=[pl.BlockSpec((tt, D), lambda i: (i, 0)),
                  pl.BlockSpec((D, D), lambda i: (0, 0)),
                  pl.BlockSpec((2 * N_HEADS, N_KEYS, N_KEYS), lambda i: (0, 0, 0))],
        out_specs=pl.BlockSpec((D, tt), lambda i: (0, i)),
        out_shape=jax.ShapeDtypeStruct((D, T), F32),
        compiler_params=_params("arbitrary"),
        name="peer_scores",
    )(u2, w_q, keys)


TOPK_SUB = 256


def _sort16_network():
    def merge(lo, hi, r):
        step = r * 2
        if step < hi - lo:
            yield from merge(lo, hi, step)
            yield from merge(lo + r, hi, step)
            yield from ((i, i + r) for i in range(lo + r, hi - r, step))
        else:
            yield (lo, lo + r)

    def sort(lo, hi):
        if hi > lo:
            mid = lo + (hi - lo) // 2
            yield from sort(lo, mid)
            yield from sort(mid + 1, hi)
            yield from merge(lo, hi, 1)

    return tuple(sort(0, TOPK - 1))


_SORT16 = _sort16_network()
_BITONIC16 = tuple((i, i + d) for d in (8, 4, 2, 1) for i in range(TOPK) if not i & d)


def _compare_exchange(v, net):
    for a, b in net:
        v[a], v[b] = jnp.maximum(v[a], v[b]), jnp.minimum(v[a], v[b])
    return v


def _top_values(s):
    assert s.shape[0] == 8 * TOPK
    v = _compare_exchange([s[8 * m:8 * m + 8, :] for m in range(TOPK)], _SORT16)
    for sh in (4, 2, 1):
        rot = [pltpu.roll(x, 8 - sh, axis=0) for x in v]
        v = _compare_exchange([jnp.maximum(v[i], rot[TOPK - 1 - i]) for i in range(TOPK)], _BITONIC16)
    return [x[0:1, :] for x in v]


def _topk_kernel(st_ref, e1_ref, e2_ref, phi_ref, v1_ref, v2_ref, ev2_ref, aux_ref, *, tl):
    n_grp = tl // LANES
    big = -NEG_BIG
    c_any = TOPK // 2

    def head(h, carry):
        r1 = pl.multiple_of(h * 2 * N_KEYS, 2 * N_KEYS)
        r2 = r1 + N_KEYS
        for q in range(tl // TOPK_SUB):
            lo = q * TOPK_SUB
            for half, vref in ((r1, v1_ref), (r2, v2_ref)):
                vals = _top_values(st_ref[pl.ds(half, N_KEYS), lo:lo + TOPK_SUB])
                for r in range(TOPK):
                    for g in range(TOPK_SUB // LANES):
                        gg = q * (TOPK_SUB // LANES) + g
                        vref[r, gg:gg + 1, :] = vals[r][:, g * LANES:(g + 1) * LANES]
        v1 = [v1_ref[r] for r in range(TOPK)]
        v2 = [v2_ref[r] for r in range(TOPK)]
        cands = [v1[r] + v2[c] for (r, c) in _CAND]
        m = functools.reduce(jnp.maximum, cands)
        for _ in range(TOPK - 1):
            m = functools.reduce(jnp.maximum, [jnp.where(cd < m, cd, NEG_BIG) for cd in cands])
        tau = m
        top = cands[0]
        z = functools.reduce(
            lambda a, b: a + b, [jnp.where(cd >= tau, jnp.exp(cd - top), 0.0) for cd in cands])
        ev2 = [jnp.exp(v2[c] - v2[0]) for c in range(TOPK)]
        for c in range(TOPK):
            ev2_ref[c] = ev2[c]
        phi_top = functools.reduce(
            jnp.minimum, [jnp.where(v1[0] + v2[c] >= tau, ev2[c], big) for c in range(c_any, TOPK)])
        aux_ref[0] = tau
        aux_ref[1] = 1.0 / z
        aux_ref[2] = phi_top
        for g in range(n_grp):
            ls = slice(g * LANES, (g + 1) * LANES)
            row = lambda ref, r: ref[r, g:g + 1, :]
            s1 = st_ref[pl.ds(r1, N_KEYS), ls]
            s2 = st_ref[pl.ds(r2, N_KEYS), ls]
            tau_row = row(aux_ref, 0)
            phi = jnp.where(s1 + row(v2_ref, 0) >= tau_row, row(ev2_ref, 0), big)
            for c in range(1, c_any):
                phi = jnp.minimum(phi, jnp.where(s1 + row(v2_ref, c) >= tau_row, row(ev2_ref, c), big))
            phi = jnp.where(s1 >= row(v1_ref, 0), jnp.minimum(phi, row(aux_ref, 2)), phi)
            e1 = jnp.where(s1 >= row(v1_ref, TOPK - 1), jnp.exp(s1 - row(v1_ref, 0)), 0.0)
            e2 = jnp.where(s2 >= row(v2_ref, TOPK - 1), jnp.exp(s2 - row(v2_ref, 0)), 0.0)
            e1_ref[h, :, ls] = e1 * row(aux_ref, 1)
            e2_ref[h, :, ls] = e2
            phi_ref[h, :, ls] = phi
        return carry

    lax.fori_loop(0, N_HEADS, head, 0)


def _topk(st, tl=512):
    T = st.shape[1]
    n_grp = tl // LANES
    e_spec = pl.BlockSpec((N_HEADS, N_KEYS, tl), lambda i: (0, 0, i))
    return pl.pallas_call(
        functools.partial(_topk_kernel, tl=tl),
        grid=(T // tl,),
        in_specs=[pl.BlockSpec((D, tl), lambda i: (0, i))],
        out_specs=[e_spec, e_spec, e_spec],
        out_shape=[jax.ShapeDtypeStruct((N_HEADS, N_KEYS, T), F32)] * 3,
        scratch_shapes=[pltpu.VMEM((TOPK, n_grp, LANES), F32), pltpu.VMEM((TOPK, n_grp, LANES), F32),
                        pltpu.VMEM((TOPK, n_grp, LANES), F32), pltpu.VMEM((3, n_grp, LANES), F32)],
        compiler_params=_params("arbitrary"),
        name="peer_topk",
    )(st)


PEER_TC = 256


PEER_IROWS = 8


def _peer_gate_tile(row, col, act_ref, e1_ref, phi_ref, e2_ref, g_ref, *, tt):
    cols = slice(col * N_KEYS, (col + 1) * N_KEYS)
    for t0 in range(0, tt, PEER_TC):
        ts = slice(t0, t0 + PEER_TC)
        w = jnp.zeros((N_KEYS, PEER_TC), F32)
        for h in range(N_HEADS):
            e1row = e1_ref[h, row:row + 1, ts]
            phirow = phi_ref[h, row:row + 1, ts]
            e2 = e2_ref[h, :, ts]
            w = w + jnp.where(e2 >= phirow, e2, 0.0) * e1row
        g_ref[ts, cols] = (w.T * act_ref[ts, cols]).astype(BF16)


def _peer_kernel(u2_ref, pu0_ref, pub_ref, pun_ref, pva_ref, pvb_ref, pvl_ref, e1_ref, phi_ref,
                 e2_ref, o_ref, acta_ref, actb_ref, ga_ref, gb_ref, ob_ref, *, tt, te, n_k):
    k = pl.program_id(1)
    half = te // 2
    n_il = half // N_KEYS
    sub = half // 2
    wide = D // 2

    def act_of(rows):
        a = lax.dot_general(u2_ref[...], rows, (((1,), (1,)), ((), ())), preferred_element_type=F32)
        return 0.5 * a * (1.0 + lax.erf(a * INV_SQRT2))

    @pl.when(k == 0)
    def _():
        o_ref[...] = jnp.zeros((tt, D), F32)
        ob_ref[...] = jnp.zeros((tt, D), F32)
        gb_ref[...] = jnp.zeros((tt, half), BF16)
        acta_ref[...] = act_of(pu0_ref[...])

    def act_job(dst_ref, src_ref, j):
        def run():
            dst_ref[:, j * sub:(j + 1) * sub] = act_of(src_ref[j * sub:(j + 1) * sub, :])
        return run

    def mix_job(acc_ref, g_ref, v_ref, j):
        def run():
            cs = slice(j * wide, (j + 1) * wide)
            acc_ref[:, cs] += jnp.dot(g_ref[...], v_ref[:, cs], preferred_element_type=F32)
        return run

    jobs_a = [act_job(actb_ref, pub_ref, 0), mix_job(ob_ref, gb_ref, pvb_ref, 0),
              act_job(actb_ref, pub_ref, 1), mix_job(ob_ref, gb_ref, pvb_ref, 1)]
    jobs_b = [act_job(acta_ref, pun_ref, 0), mix_job(o_ref, ga_ref, pva_ref, 0),
              act_job(acta_ref, pun_ref, 1), mix_job(o_ref, ga_ref, pva_ref, 1)]
    gate = functools.partial(_peer_gate_tile, e1_ref=e1_ref, phi_ref=phi_ref, e2_ref=e2_ref, tt=tt)
    for il in range(n_il):
        jobs_a[il]()
        gate(il, il, acta_ref, g_ref=ga_ref)
    for il in range(n_il):
        jobs_b[il]()
        gate(n_il + il, il, actb_ref, g_ref=gb_ref)

    @pl.when(k == n_k - 1)
    def _():
        o_ref[...] += ob_ref[...] + jnp.dot(gb_ref[...], pvl_ref[...], preferred_element_type=F32)


def _peer(u2, peer_u, peer_v, e1, phi, e2, tt=512, te=1024):
    T = u2.shape[0]
    n_exp = peer_u.shape[0]
    assert te == PEER_IROWS * N_KEYS and n_exp % te == 0
    n_k = n_exp // te
    half = te // 2
    e1r = e1.reshape(N_HEADS, n_k, PEER_IROWS, T)
    phir = phi.reshape(N_HEADS, n_k, PEER_IROWS, T)
    rows_spec = pl.BlockSpec((N_HEADS, None, PEER_IROWS, tt), lambda i, k: (0, k, 0, i))
    halfs = lambda fn: pl.BlockSpec((half, D), lambda i, k: (fn(k), 0))
    last = 2 * n_k - 1
    return pl.pallas_call(
        functools.partial(_peer_kernel, tt=tt, te=te, n_k=n_k),
        grid=(T // tt, n_k),
        in_specs=[pl.BlockSpec((tt, D), lambda i, k: (i, 0)),
                  halfs(lambda k: 0),
                  halfs(lambda k: 2 * k + 1),
                  halfs(lambda k: jnp.minimum(2 * k + 2, last - 1)),
                  halfs(lambda k: 2 * k),
                  halfs(lambda k: jnp.maximum(2 * k - 1, 0)),
                  halfs(lambda k: last),
                  rows_spec, rows_spec,
                  pl.BlockSpec((N_HEADS, N_KEYS, tt), lambda i, k: (0, 0, i))],
        out_specs=pl.BlockSpec((tt, D), lambda i, k: (i, 0)),
        out_shape=jax.ShapeDtypeStruct((T, D), F32),
        scratch_shapes=[pltpu.VMEM((tt, half), F32), pltpu.VMEM((tt, half), F32),
                        pltpu.VMEM((tt, half), BF16), pltpu.VMEM((tt, half), BF16),
                        pltpu.VMEM((tt, D), F32)],
        compiler_params=_params("arbitrary", "arbitrary"),
        name="peer_mix",
    )(u2, peer_u, peer_u, peer_u, peer_v, peer_v, peer_v, e1r, phir, e2)


def _final_kernel(f_ref, x1_ref, g2_ref, lg_ref, lb_ref, y_ref, *, alpha):
    h = alpha * x1_ref[...] + (1.0 + g2_ref[...]) * f_ref[...]
    y_ref[...] = _layernorm(h, lg_ref[...], lb_ref[...])


def _final(f, row0, x1, mod, per_token, tiles_per_seq, lg, lb, alpha, tt=256):
    T = x1.shape[0]
    spec = _mod_spec_fn(per_token, tt, tiles_per_seq)
    off = row0 // tt
    row = pl.BlockSpec((tt, D), lambda i: (i, 0))
    const = pl.BlockSpec((1, D), lambda i: (0, 0))
    return pl.pallas_call(
        functools.partial(_final_kernel, alpha=alpha),
        grid=(T // tt,),
        in_specs=[pl.BlockSpec((tt, D), lambda i: (off + i, 0)), row, spec(5), const, const],
        out_specs=row,
        out_shape=jax.ShapeDtypeStruct((T, D), F32),
        compiler_params=_params("arbitrary"),
        name="final_ln",
    )(f, x1, mod, lg, lb)


def _pad_rows(w, n):
    return jnp.pad(w, ((0, n - w.shape[0]), (0, 0)))


def _layer(xp, xs, c_all, n_p, state_a, state_b, w_mod, b_mod, w_in, conv_a_w, conv_a_b, ln_a_g,
           ln_a_b, w_a_out, conv_b_w, w_b_out, w_o, ln1_g, ln1_b, w_q, sub_keys, peer_u, peer_v,
           ln2_g, ln2_b, alpha, seq_len, n_pos):
    n_s = xs.shape[0] // n_pos
    Tp = xp.shape[0]
    row = lambda v: v.reshape(1, D)

    mod = _modulation(c_all, w_mod, b_mod)
    mod_p = mod[:n_p].reshape(n_p, 6, 1, D)
    mod_s = jnp.tile(mod[n_p:n_p + n_s].reshape(n_s, 6, D).transpose(1, 0, 2), (1, n_pos, 1))

    w_in_b = w_in.astype(BF16)
    wa = _pad_rows(conv_a_w, 32)
    wb = _pad_rows(conv_b_w, 8)
    conv_args = (wa, row(conv_a_b), row(ln_a_g), row(ln_a_b), wb)

    glu_p, cgh_p, bg_p, sga_p, sgb_p = _in_proj(xp, mod_p, False, seq_len // 512, w_in_b)
    ya_p, yb_p = _conv_prompt(glu_p, cgh_p, bg_p, *conv_args, seq_len=seq_len)
    glu_s, cgh_s, bg_s, sga_s, sgb_s = _in_proj(xs, mod_s, True, 1, w_in_b)
    slab = lambda v: v.reshape(n_pos, n_s, D)
    ya_s, yb_s = _conv_sample(slab(glu_s), slab(cgh_s), slab(bg_s), state_a.transpose(1, 0, 2),
                              state_b.transpose(1, 0, 2), *conv_args)

    proj_w = (w_a_out.astype(BF16), w_b_out.astype(BF16), w_o.astype(BF16), row(ln1_g), row(ln1_b))
    x1_p, u2_p = _out_proj(ya_p, yb_p, sga_p, sgb_p, xp, mod_p, False, seq_len // 512, *proj_w, alpha)
    x1_s, u2_s = _out_proj(ya_s.reshape(-1, D), yb_s.reshape(-1, D), sga_s, sgb_s, xs, mod_s, True, 1,
                           *proj_w, alpha)

    u2 = jnp.concatenate([u2_p, u2_s], axis=0)
    keys = sub_keys.astype(BF16).reshape(2 * N_HEADS, N_KEYS, N_KEYS)
    st = _scores(u2, w_q.astype(BF16), keys)
    e1, e2, phi = _topk(st)
    f = _peer(u2, peer_u.astype(BF16), peer_v.astype(BF16), e1, phi, e2)

    y_p = _final(f, 0, x1_p, mod_p, False, seq_len // 256, row(ln2_g), row(ln2_b), alpha)
    y_s = _final(f, Tp, x1_s, mod_s, True, 1, row(ln2_g), row(ln2_b), alpha)

    new_a_p = glu_p.reshape(n_p, seq_len, D)[:, seq_len - (CONF_W - 1):]
    new_b_p = cgh_p.reshape(n_p, seq_len, D)[:, seq_len - (SC_W - 1):]
    glu_s_sm = slab(glu_s).transpose(1, 0, 2)
    cgh_s_sm = slab(cgh_s).transpose(1, 0, 2)
    new_a_s = jnp.concatenate([state_a, glu_s_sm], axis=1)[:, -(CONF_W - 1):]
    new_b_s = jnp.concatenate([state_b, cgh_s_sm], axis=1)[:, -(SC_W - 1):]
    return y_p, y_s, new_a_p, new_b_p, new_a_s, new_b_s


def kernel(x_prompt, x_sample, state_conv_a, state_conv_b, c_prompt, c_sample, w_mod, b_mod, w_in, conv_a_w, conv_a_b, ln_a_g, ln_a_b, w_a_out, conv_b_w, w_b_out, w_o, ln1_g, ln1_b, w_q, sub_keys, peer_u, peer_v, ln2_g, ln2_b):
    n_p, seq_len, d = x_prompt.shape
    n_s, n_pos, _ = x_sample.shape
    depth = w_mod.shape[0]
    assert d == D and seq_len % 512 == 0 and (n_s * n_pos) % 512 == 0 and n_s % 32 == 0
    alpha = (2.0 * depth) ** 0.25

    c_all = jnp.concatenate([c_prompt, c_sample], axis=0)
    c_all = _pad_rows(c_all, -(-c_all.shape[0] // 8) * 8)
    xp = x_prompt.reshape(n_p * seq_len, D)
    xs = x_sample.transpose(1, 0, 2).reshape(n_pos * n_s, D)

    layer_w = (w_mod, b_mod, w_in, conv_a_w, conv_a_b, ln_a_g, ln_a_b, w_a_out, conv_b_w, w_b_out,
               w_o, ln1_g, ln1_b, w_q, sub_keys, peer_u, peer_v, ln2_g, ln2_b)
    na_p, nb_p, na_s, nb_s = [], [], [], []
    for l in range(depth):
        xp, xs, a_p, b_p, a_s, b_s = _layer(
            xp, xs, c_all, n_p, state_conv_a[l], state_conv_b[l], *[w[l] for w in layer_w],
            alpha=alpha, seq_len=seq_len, n_pos=n_pos)
        na_p.append(a_p)
        nb_p.append(b_p)
        na_s.append(a_s)
        nb_s.append(b_s)
    y_prompt = xp.reshape(n_p, seq_len, D)
    y_sample = xs.reshape(n_pos, n_s, D).transpose(1, 0, 2)
    return (y_prompt, y_sample, jnp.stack(na_p), jnp.stack(nb_p), jnp.stack(na_s), jnp.stack(nb_s))
```

```python
import functools

import jax
import jax.numpy as jnp
from jax import lax
from jax.experimental import pallas as pl
from jax.experimental.pallas import tpu as pltpu

F32 = jnp.float32
BF16 = jnp.bfloat16

D = 2048
CONF_W = 31
SC_W = 3
N_HEADS = 8
N_KEYS = 128
TOPK = 16
LN_EPS = 1e-5
INV_SQRT2 = 0.7071067811865476
NEG_BIG = -3.0e38

LANES = 128
VMEM_LIMIT = 56 * 1024 * 1024

def _params(*sem, flags=None):
    return pltpu.CompilerParams(dimension_semantics=sem, vmem_limit_bytes=VMEM_LIMIT, flags=flags)


def _layernorm(x, g, b):
    mu = jnp.mean(x, axis=-1, keepdims=True)
    xc = x - mu
    var = jnp.mean(xc * xc, axis=-1, keepdims=True)
    return xc * lax.rsqrt(var + LN_EPS) * g + b


def _mod_spec_fn(per_token, tt, tiles_per_seq):
    def spec(k):
        if per_token:
            return pl.BlockSpec((None, tt, D), lambda i, *_: (k, i, 0))
        return pl.BlockSpec((None, None, 1, D), lambda i, *_: (i // tiles_per_seq, k, 0, 0))
    return spec


def _mod_kernel(c_ref, w_ref, b_ref, o_ref):
    c = c_ref[...]
    s = (c * jax.nn.sigmoid(c)).astype(BF16)
    o_ref[...] = jnp.dot(s, w_ref[...].astype(BF16), preferred_element_type=F32) + b_ref[...]


def _modulation(c_all, w_mod, b_mod):
    n = c_all.shape[0]
    tn = 1024
    return pl.pallas_call(
        _mod_kernel,
        grid=(6 * D // tn,),
        in_specs=[pl.BlockSpec((n, D), lambda j: (0, 0)),
                  pl.BlockSpec((D, tn), lambda j: (0, j)),
                  pl.BlockSpec((1, tn), lambda j: (0, j))],
        out_specs=pl.BlockSpec((n, tn), lambda j: (0, j)),
        out_shape=jax.ShapeDtypeStruct((n, 6 * D), F32),
        compiler_params=_params("arbitrary"),
        name="modulation",
    )(c_all, w_mod, b_mod.reshape(1, 6 * D))


def _inproj_kernel(x_ref, sc_ref, sh_ref, w0, w1, w2, w3, w4, w5, w6,
                   glu_ref, cgh_ref, bg_ref, sga_ref, sgb_ref):
    u = (x_ref[...] * (1.0 + sc_ref[...]) + sh_ref[...]).astype(BF16)

    def z(w):
        return jnp.dot(u, w[...], preferred_element_type=F32)

    glu_ref[...] = z(w0) * jax.nn.sigmoid(z(w1))
    bg_ref[...] = z(w2)
    cgh_ref[...] = z(w3) * z(w4)
    sga_ref[...] = jax.nn.sigmoid(z(w5))
    sgb_ref[...] = jax.nn.sigmoid(z(w6))


def _in_proj(x2d, mod, per_token, tiles_per_seq, w_in, tt=512, tn=512):
    T = x2d.shape[0]
    spec = _mod_spec_fn(per_token, tt, tiles_per_seq)
    nj = D // tn
    w_specs = [pl.BlockSpec((D, tn), lambda i, j, g=g: (0, g * nj + j)) for g in range(7)]
    o_spec = pl.BlockSpec((tt, tn), lambda i, j: (i, j))
    return pl.pallas_call(
        _inproj_kernel,
        grid=(T // tt, nj),
        in_specs=[pl.BlockSpec((tt, D), lambda i, j: (i, 0)), spec(1), spec(0)] + w_specs,
        out_specs=[o_spec] * 5,
        out_shape=[jax.ShapeDtypeStruct((T, D), F32)] * 5,
        compiler_params=_params("arbitrary", "arbitrary"),
        name="in_proj",
    )(x2d, mod, mod, *([w_in] * 7))


CONV_CC = 256
CONV_RC = 128
HALO_A = 32
HALO_B = 8


def _conv_prompt_kernel(g_ref, gh_ref, c_ref, ch_ref, bg_ref, wa_ref, ba_ref, lg_ref, lb_ref,
                        wb_ref, ya_ref, yb_ref, ext_ref, extb_ref, pre_ref, sh_ref, *, tt, tiles_per_seq):
    i = pl.program_id(0)
    first = i % tiles_per_seq == 0

    @pl.when(first)
    def _():
        ext_ref[0:HALO_A, :] = jnp.zeros((HALO_A, D), F32)
        extb_ref[0:HALO_B, :] = jnp.zeros((HALO_B, D), F32)

    @pl.when(jnp.logical_not(first))
    def _():
        ext_ref[0:HALO_A, :] = gh_ref[...]
        extb_ref[0:HALO_B, :] = ch_ref[...]

    ext_ref[HALO_A:, :] = g_ref[...]
    extb_ref[HALO_B:, :] = c_ref[...]

    def chunk(c, carry):
        cs = pl.ds(pl.multiple_of(c * CONV_CC, CONV_CC), CONV_CC)
        wa = wa_ref[:, cs]
        wb = wb_ref[:, cs]
        bias = jnp.broadcast_to(ba_ref[:, cs], (CONV_RC, CONV_CC))
        for r in range(tt // CONV_RC):
            r0 = r * CONV_RC
            acc = bias
            lead = HALO_A - (CONF_W - 1)
            for b in range(8):
                taps = [k for k in range(CONF_W) if (lead + k) % 8 == b]
                span = (lead + taps[-1]) // 8 * 8 + CONV_RC
                if b:
                    sh_ref[b - 1, 0:span, :] = ext_ref[pl.ds(r0 + b, span), cs]
                for k in taps:
                    a8 = (lead + k) // 8 * 8
                    win = (sh_ref[b - 1, a8:a8 + CONV_RC, :] if b
                           else ext_ref[pl.ds(r0 + a8, CONV_RC), cs])
                    acc = acc + win * wa[k:k + 1, :]
            pre_ref[pl.ds(r0, CONV_RC), cs] = acc
            accb = extb_ref[pl.ds(r0 + HALO_B - (SC_W - 1), CONV_RC), cs] * wb[0:1, :]
            for k in range(1, SC_W):
                accb = accb + extb_ref[pl.ds(r0 + HALO_B - (SC_W - 1) + k, CONV_RC), cs] * wb[k:k + 1, :]
            yb_ref[pl.ds(r0, CONV_RC), cs] = (bg_ref[pl.ds(r0, CONV_RC), cs] * accb).astype(BF16)
        return carry

    lax.fori_loop(0, D // CONV_CC, chunk, 0)
    y = _layernorm(pre_ref[...], lg_ref[...], lb_ref[...])
    ya_ref[...] = (y * jax.nn.sigmoid(y)).astype(BF16)


def _conv_prompt(glu, cgh, bg, wa, ba, lg, lb, wb, seq_len, tt=256):
    T = glu.shape[0]
    tiles_per_seq = seq_len // tt
    row = pl.BlockSpec((tt, D), lambda i: (i, 0))
    const = lambda shape: pl.BlockSpec(shape, lambda i: (0, 0))
    halo_a = pl.BlockSpec((HALO_A, D), lambda i: (jnp.maximum(i * (tt // HALO_A) - 1, 0), 0))
    halo_b = pl.BlockSpec((HALO_B, D), lambda i: (jnp.maximum(i * (tt // HALO_B) - 1, 0), 0))
    return pl.pallas_call(
        functools.partial(_conv_prompt_kernel, tt=tt, tiles_per_seq=tiles_per_seq),
        grid=(T // tt,),
        in_specs=[row, halo_a, row, halo_b, row, const((32, D)), const((1, D)), const((1, D)),
                  const((1, D)), const((8, D))],
        out_specs=[row, row],
        out_shape=[jax.ShapeDtypeStruct((T, D), BF16)] * 2,
        scratch_shapes=[pltpu.VMEM((tt + HALO_A, D), F32), pltpu.VMEM((tt + HALO_B, D), F32),
                        pltpu.VMEM((tt, D), F32), pltpu.VMEM((7, CONV_RC + HALO_A, CONV_CC), F32)],
        compiler_params=_params("arbitrary"),
        name="conv_prompt",
    )(glu, glu, cgh, cgh, bg, wa, ba, lg, lb, wb)


def _conv_sample_kernel(g_ref, sa_ref, c_ref, sb_ref, bg_ref, wa_ref, ba_ref, lg_ref, lb_ref,
                        wb_ref, ya_ref, yb_ref, pre_ref, *, n_pos, bs):
    n_hist_a = CONF_W - 1
    n_hist_b = SC_W - 1

    def chunk(c, carry):
        cs = pl.ds(pl.multiple_of(c * CONV_CC, CONV_CC), CONV_CC)
        wa = wa_ref[:, cs]
        wb = wb_ref[:, cs]
        bias = jnp.broadcast_to(ba_ref[:, cs], (bs, CONV_CC))
        acc = [bias] * n_pos
        for j in range(n_hist_a + n_pos):
            row = sa_ref[j, :, cs] if j < n_hist_a else g_ref[j - n_hist_a, :, cs]
            for p in range(n_pos):
                k = j - p
                if 0 <= k < CONF_W:
                    acc[p] = acc[p] + row * wa[k:k + 1, :]
        for p in range(n_pos):
            pre_ref[p, :, cs] = acc[p]
        rows_b = [sb_ref[j, :, cs] for j in range(n_hist_b)] + [c_ref[p, :, cs] for p in range(n_pos)]
        for p in range(n_pos):
            accb = rows_b[p] * wb[0:1, :]
            for k in range(1, SC_W):
                accb = accb + rows_b[p + k] * wb[k:k + 1, :]
            yb_ref[p, :, cs] = (bg_ref[p, :, cs] * accb).astype(BF16)
        return carry

    lax.fori_loop(0, D // CONV_CC, chunk, 0)
    for p in range(n_pos):
        y = _layernorm(pre_ref[p], lg_ref[...], lb_ref[...])
        ya_ref[p] = (y * jax.nn.sigmoid(y)).astype(BF16)


def _conv_sample(glu, cgh, bg, state_a, state_b, wa, ba, lg, lb, wb, bs=32):
    n_pos, n_seq, _ = glu.shape
    slab = pl.BlockSpec((n_pos, bs, D), lambda i: (0, i, 0))
    const = lambda shape: pl.BlockSpec(shape, lambda i: (0, 0))
    return pl.pallas_call(
        functools.partial(_conv_sample_kernel, n_pos=n_pos, bs=bs),
        grid=(n_seq // bs,),
        in_specs=[slab, pl.BlockSpec((CONF_W - 1, bs, D), lambda i: (0, i, 0)),
                  slab, pl.BlockSpec((SC_W - 1, bs, D), lambda i: (0, i, 0)),
                  slab, const((32, D)), const((1, D)), const((1, D)), const((1, D)), const((8, D))],
        out_specs=[slab, slab],
        out_shape=[jax.ShapeDtypeStruct((n_pos, n_seq, D), BF16)] * 2,
        scratch_shapes=[pltpu.VMEM((n_pos, bs, D), F32)],
        compiler_params=_params("arbitrary"),
        name="conv_sample",
    )(glu, state_a, cgh, state_b, bg, wa, ba, lg, lb, wb)


def _branch_kernel(ya_ref, yb_ref, sga_ref, sgb_ref, wa_ref, wb_ref, mm_ref):
    ya = jnp.dot(ya_ref[...], wa_ref[...], preferred_element_type=F32)
    yb = jnp.dot(yb_ref[...], wb_ref[...], preferred_element_type=F32)
    mm_ref[...] = (sga_ref[...] * ya + sgb_ref[...] * yb).astype(BF16)


def _ln1_kernel(mm_ref, x_ref, g1_ref, sc2_ref, sh2_ref, wo_ref, lg_ref, lb_ref, x1_ref, u2_ref,
                *, alpha):
    m = jnp.dot(mm_ref[...], wo_ref[...], preferred_element_type=F32)
    h = alpha * x_ref[...] + (1.0 + g1_ref[...]) * m
    x1 = _layernorm(h, lg_ref[...], lb_ref[...])
    x1_ref[...] = x1
    u2_ref[...] = (x1 * (1.0 + sc2_ref[...]) + sh2_ref[...]).astype(BF16)


def _out_proj(ya_in, yb_in, sga, sgb, x2d, mod, per_token, tiles_per_seq, w_a, w_b, w_o,
              lg, lb, alpha, tt=512, tn=512):
    T = x2d.shape[0]
    row2 = pl.BlockSpec((tt, D), lambda j, i: (i, 0))
    col2 = pl.BlockSpec((tt, tn), lambda j, i: (i, j))
    wcol = pl.BlockSpec((D, tn), lambda j, i: (0, j))
    mm = pl.pallas_call(
        _branch_kernel,
        grid=(D // tn, T // tt),
        in_specs=[row2, row2, col2, col2, wcol, wcol],
        out_specs=col2,
        out_shape=jax.ShapeDtypeStruct((T, D), BF16),
        compiler_params=_params("arbitrary", "arbitrary"),
        name="branch_proj",
    )(ya_in, yb_in, sga, sgb, w_a, w_b)

    spec = _mod_spec_fn(per_token, tt, tiles_per_seq)
    row = pl.BlockSpec((tt, D), lambda i: (i, 0))
    const = pl.BlockSpec((1, D), lambda i: (0, 0))
    return pl.pallas_call(
        functools.partial(_ln1_kernel, alpha=alpha),
        grid=(T // tt,),
        in_specs=[row, row, spec(2), spec(4), spec(3), pl.BlockSpec((D, D), lambda i: (0, 0)),
                  const, const],
        out_specs=[row, row],
        out_shape=[jax.ShapeDtypeStruct((T, D), F32), jax.ShapeDtypeStruct((T, D), BF16)],
        compiler_params=_params("arbitrary"),
        name="out_proj_ln1",
    )(mm, x2d, mod, mod, mod, w_o, lg, lb)


def _score_kernel(u2_ref, wq_ref, keys_ref, st_ref):
    q = jnp.dot(u2_ref[...], wq_ref[...], preferred_element_type=F32).astype(BF16)
    for c in range(2 * N_HEADS):
        st_ref[c * N_KEYS:(c + 1) * N_KEYS, :] = lax.dot_general(
            keys_ref[c], q[:, c * N_KEYS:(c + 1) * N_KEYS], (((1,), (1,)), ((), ())),
            preferred_element_type=F32)


def _scores(u2, w_q, keys, tt=512):
    T = u2.shape[0]
    return pl.pallas_call(
        _score_kernel,
        grid=(T // tt,),
        in_specs=[pl.BlockSpec((tt, D), lambda i: (i, 0)),
                  pl.BlockSpec((D, D), lambda i: (0, 0)),
                  pl.BlockSpec((2 * N_HEADS, N_KEYS, N_KEYS), lambda i: (0, 0, 0))],
        out_specs=pl.BlockSpec((D, tt), lambda i: (0, i)),
        out_shape=jax.ShapeDtypeStruct((D, T), F32),
        compiler_params=_params("arbitrary"),
        name="peer_scores",
    )(u2, w_q, keys)


TOPK_SUB = 256


def _sort16_network():
    def merge(lo, hi, r):
        step = r * 2
        if step < hi - lo:
            yield from merge(lo, hi, step)
            yield from merge(lo + r, hi, step)
            yield from ((i, i + r) for i in range(lo + r, hi - r, step))
        else:
            yield (lo, lo + r)

    def sort(lo, hi):
        if hi > lo:
            mid = lo + (hi - lo) // 2
            yield from sort(lo, mid)
            yield from sort(mid + 1, hi)
            yield from merge(lo, hi, 1)

    return tuple(sort(0, TOPK - 1))


_SORT16 = _sort16_network()
_BITONIC16 = tuple((i, i + d) for d in (8, 4, 2, 1) for i in range(TOPK) if not i & d)


def _compare_exchange(v, net):
    for a, b in net:
        v[a], v[b] = jnp.maximum(v[a], v[b]), jnp.minimum(v[a], v[b])
    return v


def _top_values(s):
    assert s.shape[0] == 8 * TOPK
    v = _compare_exchange([s[8 * m:8 * m + 8, :] for m in range(TOPK)], _SORT16)
    for sh in (4, 2, 1):
        rot = [pltpu.roll(x, 8 - sh, axis=0) for x in v]
        v = _compare_exchange([jnp.maximum(v[i], rot[TOPK - 1 - i]) for i in range(TOPK)], _BITONIC16)
    return [x[0:1, :] for x in v]


def _topk_kernel(st_ref, e1_ref, e2_ref, phi_ref, v1_ref, v2_ref, ev2_ref, aux_ref, *, tl):
    n_grp = tl // LANES
    big = -NEG_BIG
    c_any = TOPK // 2

    def head(h, carry):
        r1 = pl.multiple_of(h * 2 * N_KEYS, 2 * N_KEYS)
        r2 = r1 + N_KEYS
        for q in range(tl // TOPK_SUB):
            lo = q * TOPK_SUB
            for half, vref in ((r1, v1_ref), (r2, v2_ref)):
                vals = _top_values(st_ref[pl.ds(half, N_KEYS), lo:lo + TOPK_SUB])
                for r in range(TOPK):
                    for g in range(TOPK_SUB // LANES):
                        gg = q * (TOPK_SUB // LANES) + g
                        vref[r, gg:gg + 1, :] = vals[r][:, g * LANES:(g + 1) * LANES]
        v1 = [v1_ref[r] for r in range(TOPK)]
        v2 = [v2_ref[r] for r in range(TOPK)]
        top = [v1[0] + v2[c] for c in range(TOPK)]
        group = []
        for r in range(1, TOPK + 1):
            vals = [v1[r] + v2[c] for c in range(TOPK // (r + 1))] if r < TOPK else []
            if group and (r == TOPK or len(group) + len(vals) > TOPK):
                group += [jnp.full_like(top[0], NEG_BIG)] * (TOPK - len(group))
                group = _compare_exchange(group, _SORT16)
                top = _compare_exchange(
                    [jnp.maximum(top[i], group[TOPK - 1 - i]) for i in range(TOPK)], _BITONIC16)
                group = []
            group += vals
        tau = top[TOPK - 1]
        z = functools.reduce(lambda a, b: a + b, [jnp.exp(t - top[0]) for t in top])
        ev2 = [jnp.exp(v2[c] - v2[0]) for c in range(TOPK)]
        for c in range(TOPK):
            ev2_ref[c] = ev2[c]
        phi_top = functools.reduce(
            jnp.minimum, [jnp.where(v1[0] + v2[c] >= tau, ev2[c], big) for c in range(c_any, TOPK)])
        aux_ref[0] = tau
        aux_ref[1] = 1.0 / z
        aux_ref[2] = phi_top
        for g in range(n_grp):
            ls = slice(g * LANES, (g + 1) * LANES)
            row = lambda ref, r: ref[r, g:g + 1, :]
            s1 = st_ref[pl.ds(r1, N_KEYS), ls]
            s2 = st_ref[pl.ds(r2, N_KEYS), ls]
            tau_row = row(aux_ref, 0)
            phi = jnp.where(s1 + row(v2_ref, 0) >= tau_row, row(ev2_ref, 0), big)
            for c in range(1, c_any):
                phi = jnp.minimum(phi, jnp.where(s1 + row(v2_ref, c) >= tau_row, row(ev2_ref, c), big))
            phi = jnp.where(s1 >= row(v1_ref, 0), jnp.minimum(phi, row(aux_ref, 2)), phi)
            e1 = jnp.where(s1 >= row(v1_ref, TOPK - 1), jnp.exp(s1 - row(v1_ref, 0)), 0.0)
            e2 = jnp.where(s2 >= row(v2_ref, TOPK - 1), jnp.exp(s2 - row(v2_ref, 0)), 0.0)
            e1_ref[h, :, ls] = e1 * row(aux_ref, 1)
            e2_ref[h, :, ls] = e2
            phi_ref[h, :, ls] = phi
        return carry

    lax.fori_loop(0, N_HEADS, head, 0)


def _topk(st, tl=512):
    T = st.shape[1]
    n_grp = tl // LANES
    e_spec = pl.BlockSpec((N_HEADS, N_KEYS, tl), lambda i: (0, 0, i))
    return pl.pallas_call(
        functools.partial(_topk_kernel, tl=tl),
        grid=(T // tl,),
        in_specs=[pl.BlockSpec((D, tl), lambda i: (0, i))],
        out_specs=[e_spec, e_spec, e_spec],
        out_shape=[jax.ShapeDtypeStruct((N_HEADS, N_KEYS, T), F32)] * 3,
        scratch_shapes=[pltpu.VMEM((TOPK, n_grp, LANES), F32), pltpu.VMEM((TOPK, n_grp, LANES), F32),
                        pltpu.VMEM((TOPK, n_grp, LANES), F32), pltpu.VMEM((3, n_grp, LANES), F32)],
        compiler_params=_params("arbitrary"),
        name="peer_topk",
    )(st)


PEER_TC = 256


PEER_IROWS = 8


def _peer_gate_tile(row, col, act_ref, e1_ref, phi_ref, e2_ref, g_ref, *, tt):
    cols = slice(col * N_KEYS, (col + 1) * N_KEYS)
    for t0 in range(0, tt, PEER_TC):
        ts = slice(t0, t0 + PEER_TC)
        w = jnp.zeros((N_KEYS, PEER_TC), F32)
        for h in range(N_HEADS):
            e1row = e1_ref[h, row:row + 1, ts]
            phirow = phi_ref[h, row:row + 1, ts]
            e2 = e2_ref[h, :, ts]
            w = w + jnp.where(e2 >= phirow, e2, 0.0) * e1row
        g_ref[ts, cols] = (w.T * act_ref[ts, cols]).astype(BF16)


def _peer_kernel(u2_ref, pu0_ref, pub_ref, pun_ref, pva_ref, pvb_ref, pvl_ref, e1_ref, phi_ref,
                 e2_ref, o_ref, acta_ref, actb_ref, ga_ref, gb_ref, ob_ref, *, tt, te, n_k):
    k = pl.program_id(1)
    half = te // 2
    n_il = half // N_KEYS
    sub = half // 2
    wide = D // 2

    def act_of(rows):
        a = lax.dot_general(u2_ref[...], rows, (((1,), (1,)), ((), ())), preferred_element_type=F32)
        return 0.5 * a * (1.0 + lax.erf(a * INV_SQRT2))

    @pl.when(k == 0)
    def _():
        o_ref[...] = jnp.zeros((tt, D), F32)
        ob_ref[...] = jnp.zeros((tt, D), F32)
        gb_ref[...] = jnp.zeros((tt, half), BF16)
        acta_ref[...] = act_of(pu0_ref[...])

    def act_job(dst_ref, src_ref, j):
        def run():
            dst_ref[:, j * sub:(j + 1) * sub] = act_of(src_ref[j * sub:(j + 1) * sub, :])
        return run

    def mix_job(acc_ref, g_ref, v_ref, j):
        def run():
            cs = slice(j * wide, (j + 1) * wide)
            acc_ref[:, cs] += jnp.dot(g_ref[...], v_ref[:, cs], preferred_element_type=F32)
        return run

    jobs_a = [act_job(actb_ref, pub_ref, 0), mix_job(ob_ref, gb_ref, pvb_ref, 0),
              act_job(actb_ref, pub_ref, 1), mix_job(ob_ref, gb_ref, pvb_ref, 1)]
    jobs_b = [act_job(acta_ref, pun_ref, 0), mix_job(o_ref, ga_ref, pva_ref, 0),
              act_job(acta_ref, pun_ref, 1), mix_job(o_ref, ga_ref, pva_ref, 1)]
    gate = functools.partial(_peer_gate_tile, e1_ref=e1_ref, phi_ref=phi_ref, e2_ref=e2_ref, tt=tt)
    for il in range(n_il):
        jobs_a[il]()
        gate(il, il, acta_ref, g_ref=ga_ref)
    for il in range(n_il):
        jobs_b[il]()
        gate(n_il + il, il, actb_ref, g_ref=gb_ref)

    @pl.when(k == n_k - 1)
    def _():
        o_ref[...] += ob_ref[...] + jnp.dot(gb_ref[...], pvl_ref[...], preferred_element_type=F32)


def _peer(u2, peer_u, peer_v, e1, phi, e2, tt=512, te=1024):
    T = u2.shape[0]
    n_exp = peer_u.shape[0]
    assert te == PEER_IROWS * N_KEYS and n_exp % te == 0
    n_k = n_exp // te
    half = te // 2
    e1r = e1.reshape(N_HEADS, n_k, PEER_IROWS, T)
    phir = phi.reshape(N_HEADS, n_k, PEER_IROWS, T)
    rows_spec = pl.BlockSpec((N_HEADS, None, PEER_IROWS, tt), lambda i, k: (0, k, 0, i))
    halfs = lambda fn: pl.BlockSpec((half, D), lambda i, k: (fn(k), 0))
    last = 2 * n_k - 1
    return pl.pallas_call(
        functools.partial(_peer_kernel, tt=tt, te=te, n_k=n_k),
        grid=(T // tt, n_k),
        in_specs=[pl.BlockSpec((tt, D), lambda i, k: (i, 0)),
                  halfs(lambda k: 0),
                  halfs(lambda k: 2 * k + 1),
                  halfs(lambda k: jnp.minimum(2 * k + 2, last - 1)),
                  halfs(lambda k: 2 * k),
                  halfs(lambda k: jnp.maximum(2 * k - 1, 0)),
                  halfs(lambda k: last),
                  rows_spec, rows_spec,
                  pl.BlockSpec((N_HEADS, N_KEYS, tt), lambda i, k: (0, 0, i))],
        out_specs=pl.BlockSpec((tt, D), lambda i, k: (i, 0)),
        out_shape=jax.ShapeDtypeStruct((T, D), F32),
        scratch_shapes=[pltpu.VMEM((tt, half), F32), pltpu.VMEM((tt, half), F32),
                        pltpu.VMEM((tt, half), BF16), pltpu.VMEM((tt, half), BF16),
                        pltpu.VMEM((tt, D), F32)],
        compiler_params=_params("arbitrary", "arbitrary"),
        name="peer_mix",
    )(u2, peer_u, peer_u, peer_u, peer_v, peer_v, peer_v, e1r, phir, e2)


def _final_kernel(f_ref, x1_ref, g2_ref, lg_ref, lb_ref, y_ref, *, alpha):
    h = alpha * x1_ref[...] + (1.0 + g2_ref[...]) * f_ref[...]
    y_ref[...] = _layernorm(h, lg_ref[...], lb_ref[...])


def _final(f, row0, x1, mod, per_token, tiles_per_seq, lg, lb, alpha, tt=256):
    T = x1.shape[0]
    spec = _mod_spec_fn(per_token, tt, tiles_per_seq)
    off = row0 // tt
    row = pl.BlockSpec((tt, D), lambda i: (i, 0))
    const = pl.BlockSpec((1, D), lambda i: (0, 0))
    return pl.pallas_call(
        functools.partial(_final_kernel, alpha=alpha),
        grid=(T // tt,),
        in_specs=[pl.BlockSpec((tt, D), lambda i: (off + i, 0)), row, spec(5), const, const],
        out_specs=row,
        out_shape=jax.ShapeDtypeStruct((T, D), F32),
        compiler_params=_params("arbitrary"),
        name="final_ln",
    )(f, x1, mod, lg, lb)


def _pad_rows(w, n):
    return jnp.pad(w, ((0, n - w.shape[0]), (0, 0)))


def _layer(xp, xs, c_all, n_p, state_a, state_b, w_mod, b_mod, w_in, conv_a_w, conv_a_b, ln_a_g,
           ln_a_b, w_a_out, conv_b_w, w_b_out, w_o, ln1_g, ln1_b, w_q, sub_keys, peer_u, peer_v,
           ln2_g, ln2_b, alpha, seq_len, n_pos):
    n_s = xs.shape[0] // n_pos
    Tp = xp.shape[0]
    row = lambda v: v.reshape(1, D)

    mod = _modulation(c_all, w_mod, b_mod)
    mod_p = mod[:n_p].reshape(n_p, 6, 1, D)
    mod_s = jnp.tile(mod[n_p:n_p + n_s].reshape(n_s, 6, D).transpose(1, 0, 2), (1, n_pos, 1))

    w_in_b = w_in.astype(BF16)
    wa = _pad_rows(conv_a_w, 32)
    wb = _pad_rows(conv_b_w, 8)
    conv_args = (wa, row(conv_a_b), row(ln_a_g), row(ln_a_b), wb)

    glu_p, cgh_p, bg_p, sga_p, sgb_p = _in_proj(xp, mod_p, False, seq_len // 512, w_in_b)
    ya_p, yb_p = _conv_prompt(glu_p, cgh_p, bg_p, *conv_args, seq_len=seq_len)
    glu_s, cgh_s, bg_s, sga_s, sgb_s = _in_proj(xs, mod_s, True, 1, w_in_b)
    slab = lambda v: v.reshape(n_pos, n_s, D)
    ya_s, yb_s = _conv_sample(slab(glu_s), slab(cgh_s), slab(bg_s), state_a.transpose(1, 0, 2),
                              state_b.transpose(1, 0, 2), *conv_args)

    proj_w = (w_a_out.astype(BF16), w_b_out.astype(BF16), w_o.astype(BF16), row(ln1_g), row(ln1_b))
    x1_p, u2_p = _out_proj(ya_p, yb_p, sga_p, sgb_p, xp, mod_p, False, seq_len // 512, *proj_w, alpha)
    x1_s, u2_s = _out_proj(ya_s.reshape(-1, D), yb_s.reshape(-1, D), sga_s, sgb_s, xs, mod_s, True, 1,
                           *proj_w, alpha)

    u2 = jnp.concatenate([u2_p, u2_s], axis=0)
    keys = sub_keys.astype(BF16).reshape(2 * N_HEADS, N_KEYS, N_KEYS)
    st = _scores(u2, w_q.astype(BF16), keys)
    e1, e2, phi = _topk(st)
    f = _peer(u2, peer_u.astype(BF16), peer_v.astype(BF16), e1, phi, e2)

    y_p = _final(f, 0, x1_p, mod_p, False, seq_len // 256, row(ln2_g), row(ln2_b), alpha)
    y_s = _final(f, Tp, x1_s, mod_s, True, 1, row(ln2_g), row(ln2_b), alpha)

    new_a_p = glu_p.reshape(n_p, seq_len, D)[:, seq_len - (CONF_W - 1):]
    new_b_p = cgh_p.reshape(n_p, seq_len, D)[:, seq_len - (SC_W - 1):]
    glu_s_sm = slab(glu_s).transpose(1, 0, 2)
    cgh_s_sm = slab(cgh_s).transpose(1, 0, 2)
    new_a_s = jnp.concatenate([state_a, glu_s_sm], axis=1)[:, -(CONF_W - 1):]
    new_b_s = jnp.concatenate([state_b, cgh_s_sm], axis=1)[:, -(SC_W - 1):]
    return y_p, y_s, new_a_p, new_b_p, new_a_s, new_b_s


def kernel(x_prompt, x_sample, state_conv_a, state_conv_b, c_prompt, c_sample, w_mod, b_mod, w_in, conv_a_w, conv_a_b, ln_a_g, ln_a_b, w_a_out, conv_b_w, w_b_out, w_o, ln1_g, ln1_b, w_q, sub_keys, peer_u, peer_v, ln2_g, ln2_b):
    n_p, seq_len, d = x_prompt.shape
    n_s, n_pos, _ = x_sample.shape
    depth = w_mod.shape[0]
    assert d == D and seq_len % 512 == 0 and (n_s * n_pos) % 512 == 0 and n_s % 32 == 0
    alpha = (2.0 * depth) ** 0.25

    c_all = jnp.concatenate([c_prompt, c_sample], axis=0)
    c_all = _pad_rows(c_all, -(-c_all.shape[0] // 8) * 8)
    xp = x_prompt.reshape(n_p * seq_len, D)
    xs = x_sample.transpose(1, 0, 2).reshape(n_pos * n_s, D)

    layer_w = (w_mod, b_mod, w_in, conv_a_w, conv_a_b, ln_a_g, ln_a_b, w_a_out, conv_b_w, w_b_out,
               w_o, ln1_g, ln1_b, w_q, sub_keys, peer_u, peer_v, ln2_g, ln2_b)
    na_p, nb_p, na_s, nb_s = [], [], [], []
    for l in range(depth):
        xp, xs, a_p, b_p, a_s, b_s = _layer(
            xp, xs, c_all, n_p, state_conv_a[l], state_conv_b[l], *[w[l] for w in layer_w],
            alpha=alpha, seq_len=seq_len, n_pos=n_pos)
        na_p.append(a_p)
        nb_p.append(b_p)
        na_s.append(a_s)
        nb_s.append(b_s)
    y_prompt = xp.reshape(n_p, seq_len, D)
    y_sample = xs.reshape(n_pos, n_s, D).transpose(1, 0, 2)
    return (y_prompt, y_sample, jnp.stack(na_p), jnp.stack(nb_p), jnp.stack(na_s), jnp.stack(nb_s))
```

```python
import functools

import jax
import jax.numpy as jnp
from jax import lax
from jax.experimental import pallas as pl
from jax.experimental.pallas import tpu as pltpu

F32 = jnp.float32
BF16 = jnp.bfloat16

D = 2048
CONF_W = 31
SC_W = 3
N_HEADS = 8
N_KEYS = 128
TOPK = 16
LN_EPS = 1e-5
INV_SQRT2 = 0.7071067811865476
NEG_BIG = -3.0e38

LANES = 128
SUBLANES = 8
TAPS_A_PAD = -(-CONF_W // SUBLANES) * SUBLANES
VMEM_LIMIT = 56 * 1024 * 1024

def _params(*sem, flags=None):
    return pltpu.CompilerParams(dimension_semantics=sem, vmem_limit_bytes=VMEM_LIMIT, flags=flags)


def _layernorm(x, g, b):
    mu = jnp.mean(x, axis=-1, keepdims=True)
    xc = x - mu
    var = jnp.mean(xc * xc, axis=-1, keepdims=True)
    return xc * lax.rsqrt(var + LN_EPS) * g + b


def _mod_spec_fn(per_token, tt, tiles_per_seq):
    def spec(k):
        if per_token:
            return pl.BlockSpec((None, tt, D), lambda i, *_: (k, i, 0))
        return pl.BlockSpec((None, None, 1, D), lambda i, *_: (i // tiles_per_seq, k, 0, 0))
    return spec


def _mod_kernel(c_ref, w_ref, b_ref, o_ref):
    c = c_ref[...]
    s = (c * jax.nn.sigmoid(c)).astype(BF16)
    o_ref[...] = jnp.dot(s, w_ref[...].astype(BF16), preferred_element_type=F32) + b_ref[...]


def _modulation(c_all, w_mod, b_mod):
    n = c_all.shape[0]
    tn = 1024
    return pl.pallas_call(
        _mod_kernel,
        grid=(6 * D // tn,),
        in_specs=[pl.BlockSpec((n, D), lambda j: (0, 0)),
                  pl.BlockSpec((D, tn), lambda j: (0, j)),
                  pl.BlockSpec((1, tn), lambda j: (0, j))],
        out_specs=pl.BlockSpec((n, tn), lambda j: (0, j)),
        out_shape=jax.ShapeDtypeStruct((n, 6 * D), F32),
        compiler_params=_params("arbitrary"),
        name="modulation",
    )(c_all, w_mod, b_mod.reshape(1, 6 * D))


def _inproj_kernel(x_ref, sc_ref, sh_ref, w0, w1, w2, w3, w4, w5, w6,
                   glu_ref, cgh_ref, bg_ref, sga_ref, sgb_ref):
    u = (x_ref[...] * (1.0 + sc_ref[...]) + sh_ref[...]).astype(BF16)

    def z(w):
        return jnp.dot(u, w[...], preferred_element_type=F32)

    glu_ref[...] = z(w0) * jax.nn.sigmoid(z(w1))
    bg_ref[...] = z(w2)
    cgh_ref[...] = z(w3) * z(w4)
    sga_ref[...] = jax.nn.sigmoid(z(w5))
    sgb_ref[...] = jax.nn.sigmoid(z(w6))


def _in_proj(x2d, mod, per_token, tiles_per_seq, w_in, tt=512, tn=512):
    T = x2d.shape[0]
    spec = _mod_spec_fn(per_token, tt, tiles_per_seq)
    nj = D // tn
    w_specs = [pl.BlockSpec((D, tn), lambda i, j, g=g: (0, g * nj + j)) for g in range(7)]
    o_spec = pl.BlockSpec((tt, tn), lambda i, j: (i, j))
    return pl.pallas_call(
        _inproj_kernel,
        grid=(T // tt, nj),
        in_specs=[pl.BlockSpec((tt, D), lambda i, j: (i, 0)), spec(1), spec(0)] + w_specs,
        out_specs=[o_spec] * 5,
        out_shape=[jax.ShapeDtypeStruct((T, D), F32)] * 5,
        compiler_params=_params("arbitrary", "arbitrary"),
        name="in_proj",
    )(x2d, mod, mod, *([w_in] * 7))


CONV_CC = 256
CONV_RC = 128
HALO_A = 32
HALO_B = SUBLANES


def _conv_prompt_kernel(g_ref, gh_ref, c_ref, ch_ref, bg_ref, wa_ref, ba_ref, lg_ref, lb_ref,
                        wb_ref, ya_ref, yb_ref, ext_ref, extb_ref, pre_ref, sh_ref, *, tt, tiles_per_seq):
    i = pl.program_id(0)
    first = i % tiles_per_seq == 0

    @pl.when(first)
    def _():
        ext_ref[0:HALO_A, :] = jnp.zeros((HALO_A, D), F32)
        extb_ref[0:HALO_B, :] = jnp.zeros((HALO_B, D), F32)

    @pl.when(jnp.logical_not(first))
    def _():
        ext_ref[0:HALO_A, :] = gh_ref[...]
        extb_ref[0:HALO_B, :] = ch_ref[...]

    ext_ref[HALO_A:, :] = g_ref[...]
    extb_ref[HALO_B:, :] = c_ref[...]

    def chunk(c, carry):
        cs = pl.ds(pl.multiple_of(c * CONV_CC, CONV_CC), CONV_CC)
        wa = wa_ref[:, cs]
        wb = wb_ref[:, cs]
        bias = jnp.broadcast_to(ba_ref[:, cs], (CONV_RC, CONV_CC))
        for r in range(tt // CONV_RC):
            r0 = r * CONV_RC
            acc = bias
            lead = HALO_A - (CONF_W - 1)
            for b in range(SUBLANES):
                taps = [k for k in range(CONF_W) if (lead + k) % SUBLANES == b]
                span = (lead + taps[-1]) // SUBLANES * SUBLANES + CONV_RC
                if b:
                    sh_ref[b - 1, 0:span, :] = ext_ref[pl.ds(r0 + b, span), cs]
                for k in taps:
                    a8 = (lead + k) // SUBLANES * SUBLANES
                    win = (sh_ref[b - 1, a8:a8 + CONV_RC, :] if b
                           else ext_ref[pl.ds(r0 + a8, CONV_RC), cs])
                    acc = acc + win * wa[k:k + 1, :]
            pre_ref[pl.ds(r0, CONV_RC), cs] = acc
            accb = extb_ref[pl.ds(r0 + HALO_B - (SC_W - 1), CONV_RC), cs] * wb[0:1, :]
            for k in range(1, SC_W):
                accb = accb + extb_ref[pl.ds(r0 + HALO_B - (SC_W - 1) + k, CONV_RC), cs] * wb[k:k + 1, :]
            yb_ref[pl.ds(r0, CONV_RC), cs] = (bg_ref[pl.ds(r0, CONV_RC), cs] * accb).astype(BF16)
        return carry

    lax.fori_loop(0, D // CONV_CC, chunk, 0)
    y = _layernorm(pre_ref[...], lg_ref[...], lb_ref[...])
    ya_ref[...] = (y * jax.nn.sigmoid(y)).astype(BF16)


def _conv_prompt(glu, cgh, bg, wa, ba, lg, lb, wb, seq_len, tt=256):
    T = glu.shape[0]
    tiles_per_seq = seq_len // tt
    row = pl.BlockSpec((tt, D), lambda i: (i, 0))
    const = lambda shape: pl.BlockSpec(shape, lambda i: (0, 0))
    halo_a = pl.BlockSpec((HALO_A, D), lambda i: (jnp.maximum(i * (tt // HALO_A) - 1, 0), 0))
    halo_b = pl.BlockSpec((HALO_B, D), lambda i: (jnp.maximum(i * (tt // HALO_B) - 1, 0), 0))
    return pl.pallas_call(
        functools.partial(_conv_prompt_kernel, tt=tt, tiles_per_seq=tiles_per_seq),
        grid=(T // tt,),
        in_specs=[row, halo_a, row, halo_b, row, const((TAPS_A_PAD, D)), const((1, D)), const((1, D)),
                  const((1, D)), const((SUBLANES, D))],
        out_specs=[row, row],
        out_shape=[jax.ShapeDtypeStruct((T, D), BF16)] * 2,
        scratch_shapes=[pltpu.VMEM((tt + HALO_A, D), F32), pltpu.VMEM((tt + HALO_B, D), F32),
                        pltpu.VMEM((tt, D), F32), pltpu.VMEM((7, CONV_RC + HALO_A, CONV_CC), F32)],
        compiler_params=_params("arbitrary"),
        name="conv_prompt",
    )(glu, glu, cgh, cgh, bg, wa, ba, lg, lb, wb)


def _conv_sample_kernel(g_ref, sa_ref, c_ref, sb_ref, bg_ref, wa_ref, ba_ref, lg_ref, lb_ref,
                        wb_ref, ya_ref, yb_ref, pre_ref, *, n_pos, bs):
    n_hist_a = CONF_W - 1
    n_hist_b = SC_W - 1

    def chunk(c, carry):
        cs = pl.ds(pl.multiple_of(c * CONV_CC, CONV_CC), CONV_CC)
        wa = wa_ref[:, cs]
        wb = wb_ref[:, cs]
        bias = jnp.broadcast_to(ba_ref[:, cs], (bs, CONV_CC))
        acc = [bias] * n_pos
        for j in range(n_hist_a + n_pos):
            row = sa_ref[j, :, cs] if j < n_hist_a else g_ref[j - n_hist_a, :, cs]
            for p in range(n_pos):
                k = j - p
                if 0 <= k < CONF_W:
                    acc[p] = acc[p] + row * wa[k:k + 1, :]
        for p in range(n_pos):
            pre_ref[p, :, cs] = acc[p]
        rows_b = [sb_ref[j, :, cs] for j in range(n_hist_b)] + [c_ref[p, :, cs] for p in range(n_pos)]
        for p in range(n_pos):
            accb = rows_b[p] * wb[0:1, :]
            for k in range(1, SC_W):
                accb = accb + rows_b[p + k] * wb[k:k + 1, :]
            yb_ref[p, :, cs] = (bg_ref[p, :, cs] * accb).astype(BF16)
        return carry

    lax.fori_loop(0, D // CONV_CC, chunk, 0)
    for p in range(n_pos):
        y = _layernorm(pre_ref[p], lg_ref[...], lb_ref[...])
        ya_ref[p] = (y * jax.nn.sigmoid(y)).astype(BF16)


def _conv_sample(glu, cgh, bg, state_a, state_b, wa, ba, lg, lb, wb, bs=32):
    n_pos, n_seq, _ = glu.shape
    slab = pl.BlockSpec((n_pos, bs, D), lambda i: (0, i, 0))
    const = lambda shape: pl.BlockSpec(shape, lambda i: (0, 0))
    return pl.pallas_call(
        functools.partial(_conv_sample_kernel, n_pos=n_pos, bs=bs),
        grid=(n_seq // bs,),
        in_specs=[slab, pl.BlockSpec((CONF_W - 1, bs, D), lambda i: (0, i, 0)),
                  slab, pl.BlockSpec((SC_W - 1, bs, D), lambda i: (0, i, 0)),
                  slab, const((TAPS_A_PAD, D)), const((1, D)), const((1, D)), const((1, D)),
                  const((SUBLANES, D))],
        out_specs=[slab, slab],
        out_shape=[jax.ShapeDtypeStruct((n_pos, n_seq, D), BF16)] * 2,
        scratch_shapes=[pltpu.VMEM((n_pos, bs, D), F32)],
        compiler_params=_params("arbitrary"),
        name="conv_sample",
    )(glu, state_a, cgh, state_b, bg, wa, ba, lg, lb, wb)


def _branch_kernel(ya_ref, yb_ref, sga_ref, sgb_ref, wa_ref, wb_ref, mm_ref):
    ya = jnp.dot(ya_ref[...], wa_ref[...], preferred_element_type=F32)
    yb = jnp.dot(yb_ref[...], wb_ref[...], preferred_element_type=F32)
    mm_ref[...] = (sga_ref[...] * ya + sgb_ref[...] * yb).astype(BF16)


def _ln1_kernel(mm_ref, x_ref, g1_ref, sc2_ref, sh2_ref, wo_ref, lg_ref, lb_ref, x1_ref, u2_ref,
                *, alpha):
    m = jnp.dot(mm_ref[...], wo_ref[...], preferred_element_type=F32)
    h = alpha * x_ref[...] + (1.0 + g1_ref[...]) * m
    x1 = _layernorm(h, lg_ref[...], lb_ref[...])
    x1_ref[...] = x1
    u2_ref[...] = (x1 * (1.0 + sc2_ref[...]) + sh2_ref[...]).astype(BF16)


def _out_proj(ya_in, yb_in, sga, sgb, x2d, mod, per_token, tiles_per_seq, w_a, w_b, w_o,
              lg, lb, alpha, tt=512, tn=512):
    T = x2d.shape[0]
    tb = min(2 * tt, T)
    row2 = pl.BlockSpec((tb, D), lambda j, i: (i, 0))
    col2 = pl.BlockSpec((tb, tn), lambda j, i: (i, j))
    wcol = pl.BlockSpec((D, tn), lambda j, i: (0, j))
    mm = pl.pallas_call(
        _branch_kernel,
        grid=(D // tn, T // tb),
        in_specs=[row2, row2, col2, col2, wcol, wcol],
        out_specs=col2,
        out_shape=jax.ShapeDtypeStruct((T, D), BF16),
        compiler_params=_params("arbitrary", "arbitrary"),
        name="branch_proj",
    )(ya_in, yb_in, sga, sgb, w_a, w_b)

    spec = _mod_spec_fn(per_token, tt, tiles_per_seq)
    row = pl.BlockSpec((tt, D), lambda i: (i, 0))
    const = pl.BlockSpec((1, D), lambda i: (0, 0))
    return pl.pallas_call(
        functools.partial(_ln1_kernel, alpha=alpha),
        grid=(T // tt,),
        in_specs=[row, row, spec(2), spec(4), spec(3), pl.BlockSpec((D, D), lambda i: (0, 0)),
                  const, const],
        out_specs=[row, row],
        out_shape=[jax.ShapeDtypeStruct((T, D), F32), jax.ShapeDtypeStruct((T, D), BF16)],
        compiler_params=_params("arbitrary"),
        name="out_proj_ln1",
    )(mm, x2d, mod, mod, mod, w_o, lg, lb)


def _score_kernel(u2_ref, wq_ref, keys_ref, st_ref):
    q = jnp.dot(u2_ref[...], wq_ref[...], preferred_element_type=F32).astype(BF16)
    for c in range(2 * N_HEADS):
        st_ref[c * N_KEYS:(c + 1) * N_KEYS, :] = lax.dot_general(
            keys_ref[c], q[:, c * N_KEYS:(c + 1) * N_KEYS], (((1,), (1,)), ((), ())),
            preferred_element_type=F32)


def _scores(u2, w_q, keys, tt=512):
    T = u2.shape[0]
    return pl.pallas_call(
        _score_kernel,
        grid=(T // tt,),
        in_specs=[pl.BlockSpec((tt, D), lambda i: (i, 0)),
                  pl.BlockSpec((D, D), lambda i: (0, 0)),
                  pl.BlockSpec((2 * N_HEADS, N_KEYS, N_KEYS), lambda i: (0, 0, 0))],
        out_specs=pl.BlockSpec((D, tt), lambda i: (0, i)),
        out_shape=jax.ShapeDtypeStruct((D, T), F32),
        compiler_params=_params("arbitrary"),
        name="peer_scores",
    )(u2, w_q, keys)


TOPK_SUB = 256


def _sort16_network():
    def merge(lo, hi, r):
        step = r * 2
        if step < hi - lo:
            yield from merge(lo, hi, step)
            yield from merge(lo + r, hi, step)
            yield from ((i, i + r) for i in range(lo + r, hi - r, step))
        else:
            yield (lo, lo + r)

    def sort(lo, hi):
        if hi > lo:
            mid = lo + (hi - lo) // 2
            yield from sort(lo, mid)
            yield from sort(mid + 1, hi)
            yield from merge(lo, hi, 1)

    return tuple(sort(0, TOPK - 1))


_SORT16 = _sort16_network()
_BITONIC16 = tuple((i, i + d) for d in (8, 4, 2, 1) for i in range(TOPK) if not i & d)


def _compare_exchange(v, net):
    for a, b in net:
        v[a], v[b] = jnp.maximum(v[a], v[b]), jnp.minimum(v[a], v[b])
    return v


def _top_values(s):
    assert s.shape[0] == SUBLANES * TOPK
    v = _compare_exchange([s[SUBLANES * m:SUBLANES * (m + 1), :] for m in range(TOPK)], _SORT16)
    for sh in (4, 2, 1):
        rot = [pltpu.roll(x, SUBLANES - sh, axis=0) for x in v]
        v = _compare_exchange([jnp.maximum(v[i], rot[TOPK - 1 - i]) for i in range(TOPK)], _BITONIC16)
    return [x[0:1, :] for x in v]


def _topk_kernel(st_ref, e1_ref, e2_ref, phi_ref, v1_ref, v2_ref, ev2_ref, aux_ref, *, tl):
    n_grp = tl // LANES
    big = -NEG_BIG
    c_any = TOPK // 2

    def head(h, carry):
        r1 = pl.multiple_of(h * 2 * N_KEYS, 2 * N_KEYS)
        r2 = r1 + N_KEYS
        for q in range(tl // TOPK_SUB):
            lo = q * TOPK_SUB
            for half, vref in ((r1, v1_ref), (r2, v2_ref)):
                vals = _top_values(st_ref[pl.ds(half, N_KEYS), lo:lo + TOPK_SUB])
                for r in range(TOPK):
                    for g in range(TOPK_SUB // LANES):
                        gg = q * (TOPK_SUB // LANES) + g
                        vref[r, gg:gg + 1, :] = vals[r][:, g * LANES:(g + 1) * LANES]
        v1 = [v1_ref[r] for r in range(TOPK)]
        v2 = [v2_ref[r] for r in range(TOPK)]
        top = [v1[0] + v2[c] for c in range(TOPK)]
        group = []
        for r in range(1, TOPK + 1):
            vals = [v1[r] + v2[c] for c in range(TOPK // (r + 1))] if r < TOPK else []
            if group and (r == TOPK or len(group) + len(vals) > TOPK):
                group += [jnp.full_like(top[0], NEG_BIG)] * (TOPK - len(group))
                group = _compare_exchange(group, _SORT16)
                top = _compare_exchange(
                    [jnp.maximum(top[i], group[TOPK - 1 - i]) for i in range(TOPK)], _BITONIC16)
                group = []
            group += vals
        tau = top[TOPK - 1]
        z = functools.reduce(lambda a, b: a + b, [jnp.exp(t - top[0]) for t in top])
        ev2 = [jnp.exp(v2[c] - v2[0]) for c in range(TOPK)]
        for c in range(TOPK):
            ev2_ref[c] = ev2[c]
        phi_top = functools.reduce(
            jnp.minimum, [jnp.where(v1[0] + v2[c] >= tau, ev2[c], big) for c in range(c_any, TOPK)])
        aux_ref[0] = tau
        aux_ref[1] = 1.0 / z
        aux_ref[2] = phi_top
        for g in range(n_grp):
            ls = slice(g * LANES, (g + 1) * LANES)
            row = lambda ref, r: ref[r, g:g + 1, :]
            s1 = st_ref[pl.ds(r1, N_KEYS), ls]
            s2 = st_ref[pl.ds(r2, N_KEYS), ls]
            tau_row = row(aux_ref, 0)
            phi = jnp.where(s1 + row(v2_ref, 0) >= tau_row, row(ev2_ref, 0), big)
            for c in range(1, c_any):
                phi = jnp.minimum(phi, jnp.where(s1 + row(v2_ref, c) >= tau_row, row(ev2_ref, c), big))
            phi = jnp.where(s1 >= row(v1_ref, 0), jnp.minimum(phi, row(aux_ref, 2)), phi)
            e1 = jnp.where(s1 >= row(v1_ref, TOPK - 1), jnp.exp(s1 - row(v1_ref, 0)), 0.0)
            e2 = jnp.where(s2 >= row(v2_ref, TOPK - 1), jnp.exp(s2 - row(v2_ref, 0)), 0.0)
            e1_ref[h, :, ls] = e1 * row(aux_ref, 1)
            e2_ref[h, :, ls] = e2
            phi_ref[h, :, ls] = phi
        return carry

    lax.fori_loop(0, N_HEADS, head, 0)


def _topk(st, tl=512):
    T = st.shape[1]
    n_grp = tl // LANES
    e_spec = pl.BlockSpec((N_HEADS, N_KEYS, tl), lambda i: (0, 0, i))
    return pl.pallas_call(
        functools.partial(_topk_kernel, tl=tl),
        grid=(T // tl,),
        in_specs=[pl.BlockSpec((D, tl), lambda i: (0, i))],
        out_specs=[e_spec, e_spec, e_spec],
        out_shape=[jax.ShapeDtypeStruct((N_HEADS, N_KEYS, T), F32)] * 3,
        scratch_shapes=[pltpu.VMEM((TOPK, n_grp, LANES), F32), pltpu.VMEM((TOPK, n_grp, LANES), F32),
                        pltpu.VMEM((TOPK, n_grp, LANES), F32), pltpu.VMEM((3, n_grp, LANES), F32)],
        compiler_params=_params("arbitrary"),
        name="peer_topk",
    )(st)


PEER_TC = 256


PEER_IROWS = SUBLANES


def _peer_gate_tile(row, col, act_ref, e1_ref, phi_ref, e2_ref, g_ref, *, tt):
    cols = slice(col * N_KEYS, (col + 1) * N_KEYS)
    for t0 in range(0, tt, PEER_TC):
        ts = slice(t0, t0 + PEER_TC)
        w = jnp.zeros((N_KEYS, PEER_TC), F32)
        for h in range(N_HEADS):
            e1row = e1_ref[h, row:row + 1, ts]
            phirow = phi_ref[h, row:row + 1, ts]
            e2 = e2_ref[h, :, ts]
            w = w + jnp.where(e2 >= phirow, e2, 0.0) * e1row
        g_ref[ts, cols] = (w.T * act_ref[ts, cols]).astype(BF16)


def _peer_kernel(u2_ref, u2n_ref, pu0_ref, pub_ref, pun_ref, pva_ref, pvb_ref, pvl_ref, e1_ref,
                 phi_ref, e2_ref, o_ref, acta_ref, actb_ref, ga_ref, gb_ref, ob_ref, *, tt, te, n_k):
    i = pl.program_id(0)
    k = pl.program_id(1)
    half = te // 2
    n_il = half // N_KEYS
    sub = half // 2
    wide = D // 2

    def act_of(tok_ref, rows):
        a = lax.dot_general(tok_ref[...], rows, (((1,), (1,)), ((), ())), preferred_element_type=F32)
        return 0.5 * a * (1.0 + lax.erf(a * INV_SQRT2))

    @pl.when(k == 0)
    def _():
        o_ref[...] = jnp.zeros((tt, D), F32)
        ob_ref[...] = jnp.zeros((tt, D), F32)
        gb_ref[...] = jnp.zeros((tt, half), BF16)

    @pl.when(jnp.logical_and(i == 0, k == 0))
    def _():
        acta_ref[...] = act_of(u2_ref, pu0_ref[...])

    def act_job(dst_ref, tok_ref, src_ref, j):
        def run():
            dst_ref[:, j * sub:(j + 1) * sub] = act_of(tok_ref, src_ref[j * sub:(j + 1) * sub, :])
        return run

    def mix_job(acc_ref, g_ref, v_ref, j):
        def run():
            cs = slice(j * wide, (j + 1) * wide)
            acc_ref[:, cs] += jnp.dot(g_ref[...], v_ref[:, cs], preferred_element_type=F32)
        return run

    jobs_a = [act_job(actb_ref, u2_ref, pub_ref, 0), mix_job(ob_ref, gb_ref, pvb_ref, 0),
              act_job(actb_ref, u2_ref, pub_ref, 1), mix_job(ob_ref, gb_ref, pvb_ref, 1)]
    jobs_b = [act_job(acta_ref, u2n_ref, pun_ref, 0), mix_job(o_ref, ga_ref, pva_ref, 0),
              act_job(acta_ref, u2n_ref, pun_ref, 1), mix_job(o_ref, ga_ref, pva_ref, 1)]
    gate = functools.partial(_peer_gate_tile, e1_ref=e1_ref, phi_ref=phi_ref, e2_ref=e2_ref, tt=tt)
    for il in range(n_il):
        jobs_a[il]()
        gate(il, il, acta_ref, g_ref=ga_ref)
    for il in range(n_il):
        jobs_b[il]()
        gate(n_il + il, il, actb_ref, g_ref=gb_ref)

    @pl.when(k == n_k - 1)
    def _():
        o_ref[...] += ob_ref[...] + jnp.dot(gb_ref[...], pvl_ref[...], preferred_element_type=F32)


def _peer(u2, peer_u, peer_v, e1, phi, e2, tt=512, te=1024):
    T = u2.shape[0]
    n_exp = peer_u.shape[0]
    assert te == PEER_IROWS * N_KEYS and n_exp % te == 0
    n_k = n_exp // te
    n_i = T // tt
    half = te // 2
    e1r = e1.reshape(N_HEADS, n_k, PEER_IROWS, T)
    phir = phi.reshape(N_HEADS, n_k, PEER_IROWS, T)
    rows_spec = pl.BlockSpec((N_HEADS, None, PEER_IROWS, tt), lambda i, k: (0, k, 0, i))
    halfs = lambda fn: pl.BlockSpec((half, D), lambda i, k: (fn(k), 0))
    last = 2 * n_k - 1
    return pl.pallas_call(
        functools.partial(_peer_kernel, tt=tt, te=te, n_k=n_k),
        grid=(T // tt, n_k),
        in_specs=[pl.BlockSpec((tt, D), lambda i, k: (i, 0)),
                  pl.BlockSpec((tt, D), lambda i, k: (jnp.minimum(i + (k + 1) // n_k, n_i - 1), 0)),
                  halfs(lambda k: 0),
                  halfs(lambda k: 2 * k + 1),
                  halfs(lambda k: (2 * k + 2) % (last + 1)),
                  halfs(lambda k: 2 * k),
                  halfs(lambda k: jnp.maximum(2 * k - 1, 0)),
                  halfs(lambda k: last),
                  rows_spec, rows_spec,
                  pl.BlockSpec((N_HEADS, N_KEYS, tt), lambda i, k: (0, 0, i))],
        out_specs=pl.BlockSpec((tt, D), lambda i, k: (i, 0)),
        out_shape=jax.ShapeDtypeStruct((T, D), F32),
        scratch_shapes=[pltpu.VMEM((tt, half), F32), pltpu.VMEM((tt, half), F32),
                        pltpu.VMEM((tt, half), BF16), pltpu.VMEM((tt, half), BF16),
                        pltpu.VMEM((tt, D), F32)],
        compiler_params=_params("arbitrary", "arbitrary"),
        name="peer_mix",
    )(u2, u2, peer_u, peer_u, peer_u, peer_v, peer_v, peer_v, e1r, phir, e2)


def _final_kernel(f_ref, x1_ref, g2_ref, lg_ref, lb_ref, y_ref, *, alpha):
    h = alpha * x1_ref[...] + (1.0 + g2_ref[...]) * f_ref[...]
    y_ref[...] = _layernorm(h, lg_ref[...], lb_ref[...])


def _final(f, row0, x1, mod, per_token, tiles_per_seq, lg, lb, alpha, tt=256):
    T = x1.shape[0]
    spec = _mod_spec_fn(per_token, tt, tiles_per_seq)
    off = row0 // tt
    row = pl.BlockSpec((tt, D), lambda i: (i, 0))
    const = pl.BlockSpec((1, D), lambda i: (0, 0))
    return pl.pallas_call(
        functools.partial(_final_kernel, alpha=alpha),
        grid=(T // tt,),
        in_specs=[pl.BlockSpec((tt, D), lambda i: (off + i, 0)), row, spec(5), const, const],
        out_specs=row,
        out_shape=jax.ShapeDtypeStruct((T, D), F32),
        compiler_params=_params("arbitrary"),
        name="final_ln",
    )(f, x1, mod, lg, lb)


def _pad_rows(w, n):
    return jnp.pad(w, ((0, n - w.shape[0]), (0, 0)))


def _layer(xp, xs, c_all, n_p, state_a, state_b, w_mod, b_mod, w_in, conv_a_w, conv_a_b, ln_a_g,
           ln_a_b, w_a_out, conv_b_w, w_b_out, w_o, ln1_g, ln1_b, w_q, sub_keys, peer_u, peer_v,
           ln2_g, ln2_b, alpha, seq_len, n_pos):
    n_s = xs.shape[0] // n_pos
    Tp = xp.shape[0]
    row = lambda v: v.reshape(1, D)

    mod = _modulation(c_all, w_mod, b_mod)
    mod_p = mod[:n_p].reshape(n_p, 6, 1, D)
    mod_s = jnp.tile(mod[n_p:n_p + n_s].reshape(n_s, 6, D).transpose(1, 0, 2), (1, n_pos, 1))

    w_in_b = w_in.astype(BF16)
    wa = _pad_rows(conv_a_w, TAPS_A_PAD)
    wb = _pad_rows(conv_b_w, SUBLANES)
    conv_args = (wa, row(conv_a_b), row(ln_a_g), row(ln_a_b), wb)

    glu_p, cgh_p, bg_p, sga_p, sgb_p = _in_proj(xp, mod_p, False, seq_len // 512, w_in_b)
    ya_p, yb_p = _conv_prompt(glu_p, cgh_p, bg_p, *conv_args, seq_len=seq_len)
    glu_s, cgh_s, bg_s, sga_s, sgb_s = _in_proj(xs, mod_s, True, 1, w_in_b)
    slab = lambda v: v.reshape(n_pos, n_s, D)
    ya_s, yb_s = _conv_sample(slab(glu_s), slab(cgh_s), slab(bg_s), state_a.transpose(1, 0, 2),
                              state_b.transpose(1, 0, 2), *conv_args)

    proj_w = (w_a_out.astype(BF16), w_b_out.astype(BF16), w_o.astype(BF16), row(ln1_g), row(ln1_b))
    x1_p, u2_p = _out_proj(ya_p, yb_p, sga_p, sgb_p, xp, mod_p, False, seq_len // 512, *proj_w, alpha)
    x1_s, u2_s = _out_proj(ya_s.reshape(-1, D), yb_s.reshape(-1, D), sga_s, sgb_s, xs, mod_s, True, 1,
                           *proj_w, alpha)

    u2 = jnp.concatenate([u2_p, u2_s], axis=0)
    keys = sub_keys.astype(BF16).reshape(2 * N_HEADS, N_KEYS, N_KEYS)
    st = _scores(u2, w_q.astype(BF16), keys)
    e1, e2, phi = _topk(st)
    f = _peer(u2, peer_u.astype(BF16), peer_v.astype(BF16), e1, phi, e2)

    y_p = _final(f, 0, x1_p, mod_p, False, seq_len // 256, row(ln2_g), row(ln2_b), alpha)
    y_s = _final(f, Tp, x1_s, mod_s, True, 1, row(ln2_g), row(ln2_b), alpha)

    new_a_p = glu_p.reshape(n_p, seq_len, D)[:, seq_len - (CONF_W - 1):]
    new_b_p = cgh_p.reshape(n_p, seq_len, D)[:, seq_len - (SC_W - 1):]
    glu_s_sm = slab(glu_s).transpose(1, 0, 2)
    cgh_s_sm = slab(cgh_s).transpose(1, 0, 2)
    new_a_s = jnp.concatenate([state_a, glu_s_sm], axis=1)[:, -(CONF_W - 1):]
    new_b_s = jnp.concatenate([state_b, cgh_s_sm], axis=1)[:, -(SC_W - 1):]
    return y_p, y_s, new_a_p, new_b_p, new_a_s, new_b_s


def kernel(x_prompt, x_sample, state_conv_a, state_conv_b, c_prompt, c_sample, w_mod, b_mod, w_in, conv_a_w, conv_a_b, ln_a_g, ln_a_b, w_a_out, conv_b_w, w_b_out, w_o, ln1_g, ln1_b, w_q, sub_keys, peer_u, peer_v, ln2_g, ln2_b):
    n_p, seq_len, d = x_prompt.shape
    n_s, n_pos, _ = x_sample.shape
    depth = w_mod.shape[0]
    assert d == D and seq_len % 512 == 0 and (n_s * n_pos) % 512 == 0 and n_s % 32 == 0
    alpha = (2.0 * depth) ** 0.25

    c_all = jnp.concatenate([c_prompt, c_sample], axis=0)
    c_all = _pad_rows(c_all, -(-c_all.shape[0] // SUBLANES) * SUBLANES)
    xp = x_prompt.reshape(n_p * seq_len, D)
    xs = x_sample.transpose(1, 0, 2).reshape(n_pos * n_s, D)

    layer_w = (w_mod, b_mod, w_in, conv_a_w, conv_a_b, ln_a_g, ln_a_b, w_a_out, conv_b_w, w_b_out,
               w_o, ln1_g, ln1_b, w_q, sub_keys, peer_u, peer_v, ln2_g, ln2_b)
    na_p, nb_p, na_s, nb_s = [], [], [], []
    for l in range(depth):
        xp, xs, a_p, b_p, a_s, b_s = _layer(
            xp, xs, c_all, n_p, state_conv_a[l], state_conv_b[l], *[w[l] for w in layer_w],
            alpha=alpha, seq_len=seq_len, n_pos=n_pos)
        na_p.append(a_p)
        nb_p.append(b_p)
        na_s.append(a_s)
        nb_s.append(b_s)
    y_prompt = xp.reshape(n_p, seq_len, D)
    y_sample = xs.reshape(n_pos, n_s, D).transpose(1, 0, 2)
    return (y_prompt, y_sample, jnp.stack(na_p), jnp.stack(nb_p), jnp.stack(na_s), jnp.stack(nb_s))
```

```python
import functools

import jax
import jax.numpy as jnp
from jax import lax
from jax.experimental import pallas as pl
from jax.experimental.pallas import tpu as pltpu

F32 = jnp.float32
BF16 = jnp.bfloat16

D = 2048
CONF_W = 31
SC_W = 3
N_HEADS = 8
N_KEYS = 128
TOPK = 16
LN_EPS = 1e-5
INV_SQRT2 = 0.7071067811865476
NEG_BIG = -3.0e38

LANES = 128
SUBLANES = 8
TAPS_A_PAD = -(-CONF_W // SUBLANES) * SUBLANES
VMEM_LIMIT = 56 * 1024 * 1024

def _params(*sem, flags=None):
    return pltpu.CompilerParams(dimension_semantics=sem, vmem_limit_bytes=VMEM_LIMIT, flags=flags)


def _layernorm(x, g, b):
    mu = jnp.mean(x, axis=-1, keepdims=True)
    xc = x - mu
    var = jnp.mean(xc * xc, axis=-1, keepdims=True)
    return xc * lax.rsqrt(var + LN_EPS) * g + b


def _mod_spec_fn(per_token, tt, tiles_per_seq):
    def spec(k):
        if per_token:
            return pl.BlockSpec((None, tt, D), lambda i, *_: (k, i, 0))
        return pl.BlockSpec((None, None, 1, D), lambda i, *_: (i // tiles_per_seq, k, 0, 0))
    return spec


def _mod_kernel(c_ref, w_ref, b_ref, o_ref):
    c = c_ref[...]
    s = (c * jax.nn.sigmoid(c)).astype(BF16)
    o_ref[...] = jnp.dot(s, w_ref[...].astype(BF16), preferred_element_type=F32) + b_ref[...]


def _modulation(c_all, w_mod, b_mod):
    n = c_all.shape[0]
    tn = 1024
    return pl.pallas_call(
        _mod_kernel,
        grid=(6 * D // tn,),
        in_specs=[pl.BlockSpec((n, D), lambda j: (0, 0)),
                  pl.BlockSpec((D, tn), lambda j: (0, j)),
                  pl.BlockSpec((1, tn), lambda j: (0, j))],
        out_specs=pl.BlockSpec((n, tn), lambda j: (0, j)),
        out_shape=jax.ShapeDtypeStruct((n, 6 * D), F32),
        compiler_params=_params("arbitrary"),
        name="modulation",
    )(c_all, w_mod, b_mod.reshape(1, 6 * D))


def _inproj_kernel(x_ref, sc_ref, sh_ref, w0, w1, w2, w3, w4, w5, w6,
                   glu_ref, cgh_ref, bg_ref, sga_ref, sgb_ref):
    u = (x_ref[...] * (1.0 + sc_ref[...]) + sh_ref[...]).astype(BF16)

    def z(w):
        return jnp.dot(u, w[...], preferred_element_type=F32)

    glu_ref[...] = z(w0) * jax.nn.sigmoid(z(w1))
    bg_ref[...] = z(w2)
    cgh_ref[...] = z(w3) * z(w4)
    sga_ref[...] = jax.nn.sigmoid(z(w5))
    sgb_ref[...] = jax.nn.sigmoid(z(w6))


def _in_proj(x2d, mod, per_token, tiles_per_seq, w_in, tt=512, tn=512):
    T = x2d.shape[0]
    spec = _mod_spec_fn(per_token, tt, tiles_per_seq)
    nj = D // tn
    w_specs = [pl.BlockSpec((D, tn), lambda i, j, g=g: (0, g * nj + j)) for g in range(7)]
    o_spec = pl.BlockSpec((tt, tn), lambda i, j: (i, j))
    return pl.pallas_call(
        _inproj_kernel,
        grid=(T // tt, nj),
        in_specs=[pl.BlockSpec((tt, D), lambda i, j: (i, 0)), spec(1), spec(0)] + w_specs,
        out_specs=[o_spec] * 5,
        out_shape=[jax.ShapeDtypeStruct((T, D), F32)] * 5,
        compiler_params=_params("arbitrary", "arbitrary"),
        name="in_proj",
    )(x2d, mod, mod, *([w_in] * 7))


CONV_CC = 256
CONV_RC = 128
HALO_A = 32
HALO_B = SUBLANES


def _conv_prompt_kernel(g_ref, gh_ref, c_ref, ch_ref, bg_ref, wa_ref, ba_ref, lg_ref, lb_ref,
                        wb_ref, ya_ref, yb_ref, ext_ref, extb_ref, pre_ref, sh_ref, *, tt, tiles_per_seq):
    i = pl.program_id(0)
    first = i % tiles_per_seq == 0

    @pl.when(first)
    def _():
        ext_ref[0:HALO_A, :] = jnp.zeros((HALO_A, D), F32)
        extb_ref[0:HALO_B, :] = jnp.zeros((HALO_B, D), F32)

    @pl.when(jnp.logical_not(first))
    def _():
        ext_ref[0:HALO_A, :] = gh_ref[...]
        extb_ref[0:HALO_B, :] = ch_ref[...]

    ext_ref[HALO_A:, :] = g_ref[...]
    extb_ref[HALO_B:, :] = c_ref[...]

    def chunk(c, carry):
        cs = pl.ds(pl.multiple_of(c * CONV_CC, CONV_CC), CONV_CC)
        wa = wa_ref[:, cs]
        wb = wb_ref[:, cs]
        bias = jnp.broadcast_to(ba_ref[:, cs], (CONV_RC, CONV_CC))
        for r in range(tt // CONV_RC):
            r0 = r * CONV_RC
            acc = bias
            lead = HALO_A - (CONF_W - 1)
            for b in range(SUBLANES):
                taps = [k for k in range(CONF_W) if (lead + k) % SUBLANES == b]
                span = (lead + taps[-1]) // SUBLANES * SUBLANES + CONV_RC
                if b:
                    sh_ref[b - 1, 0:span, :] = ext_ref[pl.ds(r0 + b, span), cs]
                for k in taps:
                    a8 = (lead + k) // SUBLANES * SUBLANES
                    win = (sh_ref[b - 1, a8:a8 + CONV_RC, :] if b
                           else ext_ref[pl.ds(r0 + a8, CONV_RC), cs])
                    acc = acc + win * wa[k:k + 1, :]
            pre_ref[pl.ds(r0, CONV_RC), cs] = acc
            accb = extb_ref[pl.ds(r0 + HALO_B - (SC_W - 1), CONV_RC), cs] * wb[0:1, :]
            for k in range(1, SC_W):
                accb = accb + extb_ref[pl.ds(r0 + HALO_B - (SC_W - 1) + k, CONV_RC), cs] * wb[k:k + 1, :]
            yb_ref[pl.ds(r0, CONV_RC), cs] = (bg_ref[pl.ds(r0, CONV_RC), cs] * accb).astype(BF16)
        return carry

    lax.fori_loop(0, D // CONV_CC, chunk, 0)
    y = _layernorm(pre_ref[...], lg_ref[...], lb_ref[...])
    ya_ref[...] = (y * jax.nn.sigmoid(y)).astype(BF16)


def _conv_prompt(glu, cgh, bg, wa, ba, lg, lb, wb, seq_len, tt=256):
    T = glu.shape[0]
    tiles_per_seq = seq_len // tt
    row = pl.BlockSpec((tt, D), lambda i: (i, 0))
    const = lambda shape: pl.BlockSpec(shape, lambda i: (0, 0))
    halo_a = pl.BlockSpec((HALO_A, D), lambda i: (jnp.maximum(i * (tt // HALO_A) - 1, 0), 0))
    halo_b = pl.BlockSpec((HALO_B, D), lambda i: (jnp.maximum(i * (tt // HALO_B) - 1, 0), 0))
    return pl.pallas_call(
        functools.partial(_conv_prompt_kernel, tt=tt, tiles_per_seq=tiles_per_seq),
        grid=(T // tt,),
        in_specs=[row, halo_a, row, halo_b, row, const((TAPS_A_PAD, D)), const((1, D)), const((1, D)),
                  const((1, D)), const((SUBLANES, D))],
        out_specs=[row, row],
        out_shape=[jax.ShapeDtypeStruct((T, D), BF16)] * 2,
        scratch_shapes=[pltpu.VMEM((tt + HALO_A, D), F32), pltpu.VMEM((tt + HALO_B, D), F32),
                        pltpu.VMEM((tt, D), F32), pltpu.VMEM((7, CONV_RC + HALO_A, CONV_CC), F32)],
        compiler_params=_params("arbitrary"),
        name="conv_prompt",
    )(glu, glu, cgh, cgh, bg, wa, ba, lg, lb, wb)


def _conv_sample_kernel(g_ref, sa_ref, c_ref, sb_ref, bg_ref, wa_ref, ba_ref, lg_ref, lb_ref,
                        wb_ref, ya_ref, yb_ref, pre_ref, *, n_pos, bs):
    n_hist_a = CONF_W - 1
    n_hist_b = SC_W - 1

    def chunk(c, carry):
        cs = pl.ds(pl.multiple_of(c * CONV_CC, CONV_CC), CONV_CC)
        wa = wa_ref[:, cs]
        wb = wb_ref[:, cs]
        bias = jnp.broadcast_to(ba_ref[:, cs], (bs, CONV_CC))
        acc = [bias] * n_pos
        for j in range(n_hist_a + n_pos):
            row = sa_ref[j, :, cs] if j < n_hist_a else g_ref[j - n_hist_a, :, cs]
            for p in range(n_pos):
                k = j - p
                if 0 <= k < CONF_W:
                    acc[p] = acc[p] + row * wa[k:k + 1, :]
        for p in range(n_pos):
            pre_ref[p, :, cs] = acc[p]
        rows_b = [sb_ref[j, :, cs] for j in range(n_hist_b)] + [c_ref[p, :, cs] for p in range(n_pos)]
        for p in range(n_pos):
            accb = rows_b[p] * wb[0:1, :]
            for k in range(1, SC_W):
                accb = accb + rows_b[p + k] * wb[k:k + 1, :]
            yb_ref[p, :, cs] = (bg_ref[p, :, cs] * accb).astype(BF16)
        return carry

    lax.fori_loop(0, D // CONV_CC, chunk, 0)
    for p in range(n_pos):
        y = _layernorm(pre_ref[p], lg_ref[...], lb_ref[...])
        ya_ref[p] = (y * jax.nn.sigmoid(y)).astype(BF16)


def _conv_sample(glu, cgh, bg, state_a, state_b, wa, ba, lg, lb, wb, bs=32):
    n_pos, n_seq, _ = glu.shape
    slab = pl.BlockSpec((n_pos, bs, D), lambda i: (0, i, 0))
    const = lambda shape: pl.BlockSpec(shape, lambda i: (0, 0))
    return pl.pallas_call(
        functools.partial(_conv_sample_kernel, n_pos=n_pos, bs=bs),
        grid=(n_seq // bs,),
        in_specs=[slab, pl.BlockSpec((CONF_W - 1, bs, D), lambda i: (0, i, 0)),
                  slab, pl.BlockSpec((SC_W - 1, bs, D), lambda i: (0, i, 0)),
                  slab, const((TAPS_A_PAD, D)), const((1, D)), const((1, D)), const((1, D)),
                  const((SUBLANES, D))],
        out_specs=[slab, slab],
        out_shape=[jax.ShapeDtypeStruct((n_pos, n_seq, D), BF16)] * 2,
        scratch_shapes=[pltpu.VMEM((n_pos, bs, D), F32)],
        compiler_params=_params("arbitrary"),
        name="conv_sample",
    )(glu, state_a, cgh, state_b, bg, wa, ba, lg, lb, wb)


def _branch_kernel(ya_ref, yb_ref, sga_ref, sgb_ref, wa_ref, wb_ref, mm_ref, wab_ref, wbb_ref):
    @pl.when(pl.program_id(1) == 0)
    def _():
        wab_ref[...] = wa_ref[...].astype(BF16)
        wbb_ref[...] = wb_ref[...].astype(BF16)

    ya = jnp.dot(ya_ref[...], wab_ref[...], preferred_element_type=F32)
    yb = jnp.dot(yb_ref[...], wbb_ref[...], preferred_element_type=F32)
    mm_ref[...] = (sga_ref[...] * ya + sgb_ref[...] * yb).astype(BF16)


def _ln1_kernel(mm_ref, x_ref, g1_ref, sc2_ref, sh2_ref, wo_ref, lg_ref, lb_ref, x1_ref, u2_ref,
                *, alpha, tt):
    for hlf in range(2):
        rows = slice(hlf * tt // 2, (hlf + 1) * tt // 2)
        mod = lambda ref: ref[...] if ref.shape[0] == 1 else ref[rows, :]
        m = jnp.dot(mm_ref[rows, :], wo_ref[...], preferred_element_type=F32)
        h = alpha * x_ref[rows, :] + (1.0 + mod(g1_ref)) * m
        x1 = _layernorm(h, lg_ref[...], lb_ref[...])
        x1_ref[rows, :] = x1
        u2_ref[rows, :] = (x1 * (1.0 + mod(sc2_ref)) + mod(sh2_ref)).astype(BF16)


def _out_proj(ya_in, yb_in, sga, sgb, x2d, mod, per_token, tiles_per_seq, w_a, w_b, w_o,
              lg, lb, alpha, tt=512, tn=512):
    T = x2d.shape[0]
    tb = min(2 * tt, T)
    row2 = pl.BlockSpec((tb, D), lambda j, i: (i, 0))
    col2 = pl.BlockSpec((tb, tn), lambda j, i: (i, j))
    wcol = pl.BlockSpec((D, tn), lambda j, i: (0, j))
    mm = pl.pallas_call(
        _branch_kernel,
        grid=(D // tn, T // tb),
        in_specs=[row2, row2, col2, col2, wcol, wcol],
        out_specs=col2,
        out_shape=jax.ShapeDtypeStruct((T, D), BF16),
        scratch_shapes=[pltpu.VMEM((D, tn), BF16), pltpu.VMEM((D, tn), BF16)],
        compiler_params=_params("arbitrary", "arbitrary"),
        name="branch_proj",
    )(ya_in, yb_in, sga, sgb, w_a, w_b)

    spec = _mod_spec_fn(per_token, tt, tiles_per_seq)
    row = pl.BlockSpec((tt, D), lambda i: (i, 0))
    const = pl.BlockSpec((1, D), lambda i: (0, 0))
    return pl.pallas_call(
        functools.partial(_ln1_kernel, alpha=alpha, tt=tt),
        grid=(T // tt,),
        in_specs=[row, row, spec(2), spec(4), spec(3), pl.BlockSpec((D, D), lambda i: (0, 0)),
                  const, const],
        out_specs=[row, row],
        out_shape=[jax.ShapeDtypeStruct((T, D), F32), jax.ShapeDtypeStruct((T, D), BF16)],
        compiler_params=_params("arbitrary"),
        name="out_proj_ln1",
    )(mm, x2d, mod, mod, mod, w_o, lg, lb)


def _score_kernel(u2_ref, wq_ref, keys_ref, st_ref):
    q = jnp.dot(u2_ref[...], wq_ref[...], preferred_element_type=F32).astype(BF16)
    for c in range(2 * N_HEADS):
        st_ref[c * N_KEYS:(c + 1) * N_KEYS, :] = lax.dot_general(
            keys_ref[c], q[:, c * N_KEYS:(c + 1) * N_KEYS], (((1,), (1,)), ((), ())),
            preferred_element_type=F32)


def _scores(u2, w_q, keys, tt=512):
    T = u2.shape[0]
    return pl.pallas_call(
        _score_kernel,
        grid=(T // tt,),
        in_specs=[pl.BlockSpec((tt, D), lambda i: (i, 0)),
                  pl.BlockSpec((D, D), lambda i: (0, 0)),
                  pl.BlockSpec((2 * N_HEADS, N_KEYS, N_KEYS), lambda i: (0, 0, 0))],
        out_specs=pl.BlockSpec((D, tt), lambda i: (0, i)),
        out_shape=jax.ShapeDtypeStruct((D, T), F32),
        compiler_params=_params("arbitrary"),
        name="peer_scores",
    )(u2, w_q, keys)


TOPK_SUB = 256


def _sort16_network():
    def merge(lo, hi, r):
        step = r * 2
        if step < hi - lo:
            yield from merge(lo, hi, step)
            yield from merge(lo + r, hi, step)
            yield from ((i, i + r) for i in range(lo + r, hi - r, step))
        else:
            yield (lo, lo + r)

    def sort(lo, hi):
        if hi > lo:
            mid = lo + (hi - lo) // 2
            yield from sort(lo, mid)
            yield from sort(mid + 1, hi)
            yield from merge(lo, hi, 1)

    return tuple(sort(0, TOPK - 1))


_SORT16 = _sort16_network()
_BITONIC16 = tuple((i, i + d) for d in (8, 4, 2, 1) for i in range(TOPK) if not i & d)


def _compare_exchange(v, net):
    for a, b in net:
        v[a], v[b] = jnp.maximum(v[a], v[b]), jnp.minimum(v[a], v[b])
    return v


def _top_values(s):
    assert s.shape[0] == SUBLANES * TOPK
    v = _compare_exchange([s[SUBLANES * m:SUBLANES * (m + 1), :] for m in range(TOPK)], _SORT16)
    for sh in (4, 2, 1):
        rot = [pltpu.roll(x, SUBLANES - sh, axis=0) for x in v]
        v = _compare_exchange([jnp.maximum(v[i], rot[TOPK - 1 - i]) for i in range(TOPK)], _BITONIC16)
    return [x[0:1, :] for x in v]


def _topk_kernel(st_ref, e1_ref, e2_ref, phi_ref, v1_ref, v2_ref, ev2_ref, aux_ref, *, tl):
    n_grp = tl // LANES
    big = -NEG_BIG
    c_any = TOPK // 2

    def head(h, carry):
        r1 = pl.multiple_of(h * 2 * N_KEYS, 2 * N_KEYS)
        r2 = r1 + N_KEYS
        for q in range(tl // TOPK_SUB):
            lo = q * TOPK_SUB
            for half, vref in ((r1, v1_ref), (r2, v2_ref)):
                vals = _top_values(st_ref[pl.ds(half, N_KEYS), lo:lo + TOPK_SUB])
                for r in range(TOPK):
                    for g in range(TOPK_SUB // LANES):
                        gg = q * (TOPK_SUB // LANES) + g
                        vref[r, gg:gg + 1, :] = vals[r][:, g * LANES:(g + 1) * LANES]
        v1 = [v1_ref[r] for r in range(TOPK)]
        v2 = [v2_ref[r] for r in range(TOPK)]
        top = [v1[0] + v2[c] for c in range(TOPK)]
        group = []
        for r in range(1, TOPK + 1):
            vals = [v1[r] + v2[c] for c in range(TOPK // (r + 1))] if r < TOPK else []
            if group and (r == TOPK or len(group) + len(vals) > TOPK):
                group += [jnp.full_like(top[0], NEG_BIG)] * (TOPK - len(group))
                group = _compare_exchange(group, _SORT16)
                top = _compare_exchange(
                    [jnp.maximum(top[i], group[TOPK - 1 - i]) for i in range(TOPK)], _BITONIC16)
                group = []
            group += vals
        tau = top[TOPK - 1]
        z = functools.reduce(lambda a, b: a + b, [jnp.exp(t - top[0]) for t in top])
        ev2 = [jnp.exp(v2[c] - v2[0]) for c in range(TOPK)]
        for c in range(TOPK):
            ev2_ref[c] = ev2[c]
        phi_top = functools.reduce(
            jnp.minimum, [jnp.where(v1[0] + v2[c] >= tau, ev2[c], big) for c in range(c_any, TOPK)])
        aux_ref[0] = tau
        aux_ref[1] = 1.0 / z
        aux_ref[2] = phi_top
        for g in range(n_grp):
            ls = slice(g * LANES, (g + 1) * LANES)
            row = lambda ref, r: ref[r, g:g + 1, :]
            s1 = st_ref[pl.ds(r1, N_KEYS), ls]
            s2 = st_ref[pl.ds(r2, N_KEYS), ls]
            tau_row = row(aux_ref, 0)
            phi = jnp.where(s1 + row(v2_ref, 0) >= tau_row, row(ev2_ref, 0), big)
            for c in range(1, c_any):
                phi = jnp.minimum(phi, jnp.where(s1 + row(v2_ref, c) >= tau_row, row(ev2_ref, c), big))
            phi = jnp.where(s1 >= row(v1_ref, 0), jnp.minimum(phi, row(aux_ref, 2)), phi)
            e1 = jnp.where(s1 >= row(v1_ref, TOPK - 1), jnp.exp(s1 - row(v1_ref, 0)), 0.0)
            e2 = jnp.where(s2 >= row(v2_ref, TOPK - 1), jnp.exp(s2 - row(v2_ref, 0)), 0.0)
            e1_ref[h, :, ls] = e1 * row(aux_ref, 1)
            e2_ref[h, :, ls] = e2
            phi_ref[h, :, ls] = phi
        return carry

    lax.fori_loop(0, N_HEADS, head, 0)


def _topk(st, tl=512):
    T = st.shape[1]
    n_grp = tl // LANES
    e_spec = pl.BlockSpec((N_HEADS, N_KEYS, tl), lambda i: (0, 0, i))
    return pl.pallas_call(
        functools.partial(_topk_kernel, tl=tl),
        grid=(T // tl,),
        in_specs=[pl.BlockSpec((D, tl), lambda i: (0, i))],
        out_specs=[e_spec, e_spec, e_spec],
        out_shape=[jax.ShapeDtypeStruct((N_HEADS, N_KEYS, T), F32)] * 3,
        scratch_shapes=[pltpu.VMEM((TOPK, n_grp, LANES), F32), pltpu.VMEM((TOPK, n_grp, LANES), F32),
                        pltpu.VMEM((TOPK, n_grp, LANES), F32), pltpu.VMEM((3, n_grp, LANES), F32)],
        compiler_params=_params("arbitrary"),
        name="peer_topk",
    )(st)


PEER_TC = 256


PEER_IROWS = SUBLANES


def _peer_gate_tile(row, col, act_ref, e1_ref, phi_ref, e2_ref, g_ref, *, tt):
    cols = slice(col * N_KEYS, (col + 1) * N_KEYS)
    for t0 in range(0, tt, PEER_TC):
        ts = slice(t0, t0 + PEER_TC)
        w = jnp.zeros((N_KEYS, PEER_TC), F32)
        for h in range(N_HEADS):
            e1row = e1_ref[h, row:row + 1, ts]
            phirow = phi_ref[h, row:row + 1, ts]
            e2 = e2_ref[h, :, ts]
            w = w + jnp.where(e2 >= phirow, e2, 0.0) * e1row
        g_ref[ts, cols] = (w.T * act_ref[ts, cols]).astype(BF16)


def _peer_kernel(u2_ref, u2n_ref, pu0_ref, pub_ref, pun_ref, pva_ref, pvb_ref, pvl_ref, e1_ref,
                 phi_ref, e2_ref, o_ref, acta_ref, actb_ref, ga_ref, gb_ref, ob_ref, *, tt, te, n_k):
    i = pl.program_id(0)
    k = pl.program_id(1)
    half = te // 2
    n_il = half // N_KEYS
    sub = half // 2
    wide = D // 2

    def act_of(tok_ref, rows):
        a = lax.dot_general(tok_ref[...], rows, (((1,), (1,)), ((), ())), preferred_element_type=F32)
        return 0.5 * a * (1.0 + lax.erf(a * INV_SQRT2))

    @pl.when(k == 0)
    def _():
        o_ref[...] = jnp.zeros((tt, D), F32)
        ob_ref[...] = jnp.zeros((tt, D), F32)
        gb_ref[...] = jnp.zeros((tt, half), BF16)

    @pl.when(jnp.logical_and(i == 0, k == 0))
    def _():
        acta_ref[...] = act_of(u2_ref, pu0_ref[...])

    def act_job(dst_ref, tok_ref, src_ref, j):
        def run():
            dst_ref[:, j * sub:(j + 1) * sub] = act_of(tok_ref, src_ref[j * sub:(j + 1) * sub, :])
        return run

    def mix_job(acc_ref, g_ref, v_ref, j):
        def run():
            cs = slice(j * wide, (j + 1) * wide)
            acc_ref[:, cs] += jnp.dot(g_ref[...], v_ref[:, cs], preferred_element_type=F32)
        return run

    jobs_a = [act_job(actb_ref, u2_ref, pub_ref, 0), mix_job(ob_ref, gb_ref, pvb_ref, 0),
              act_job(actb_ref, u2_ref, pub_ref, 1), mix_job(ob_ref, gb_ref, pvb_ref, 1)]
    jobs_b = [act_job(acta_ref, u2n_ref, pun_ref, 0), mix_job(o_ref, ga_ref, pva_ref, 0),
              act_job(acta_ref, u2n_ref, pun_ref, 1), mix_job(o_ref, ga_ref, pva_ref, 1)]
    gate = functools.partial(_peer_gate_tile, e1_ref=e1_ref, phi_ref=phi_ref, e2_ref=e2_ref, tt=tt)
    for il in range(n_il):
        jobs_a[il]()
        gate(il, il, acta_ref, g_ref=ga_ref)
    for il in range(n_il):
        jobs_b[il]()
        gate(n_il + il, il, actb_ref, g_ref=gb_ref)

    @pl.when(k == n_k - 1)
    def _():
        o_ref[...] += ob_ref[...] + jnp.dot(gb_ref[...], pvl_ref[...], preferred_element_type=F32)


def _peer(u2, peer_u, peer_v, e1, phi, e2, tt=512, te=1024):
    T = u2.shape[0]
    n_exp = peer_u.shape[0]
    assert te == PEER_IROWS * N_KEYS and n_exp % te == 0
    n_k = n_exp // te
    n_i = T // tt
    half = te // 2
    e1r = e1.reshape(N_HEADS, n_k, PEER_IROWS, T)
    phir = phi.reshape(N_HEADS, n_k, PEER_IROWS, T)
    rows_spec = pl.BlockSpec((N_HEADS, None, PEER_IROWS, tt), lambda i, k: (0, k, 0, i))
    halfs = lambda fn: pl.BlockSpec((half, D), lambda i, k: (fn(k), 0))
    last = 2 * n_k - 1
    return pl.pallas_call(
        functools.partial(_peer_kernel, tt=tt, te=te, n_k=n_k),
        grid=(T // tt, n_k),
        in_specs=[pl.BlockSpec((tt, D), lambda i, k: (i, 0)),
                  pl.BlockSpec((tt, D), lambda i, k: (jnp.minimum(i + (k + 1) // n_k, n_i - 1), 0)),
                  halfs(lambda k: 0),
                  halfs(lambda k: 2 * k + 1),
                  halfs(lambda k: (2 * k + 2) % (last + 1)),
                  halfs(lambda k: 2 * k),
                  halfs(lambda k: jnp.maximum(2 * k - 1, 0)),
                  halfs(lambda k: last),
                  rows_spec, rows_spec,
                  pl.BlockSpec((N_HEADS, N_KEYS, tt), lambda i, k: (0, 0, i))],
        out_specs=pl.BlockSpec((tt, D), lambda i, k: (i, 0)),
        out_shape=jax.ShapeDtypeStruct((T, D), F32),
        scratch_shapes=[pltpu.VMEM((tt, half), F32), pltpu.VMEM((tt, half), F32),
                        pltpu.VMEM((tt, half), BF16), pltpu.VMEM((tt, half), BF16),
                        pltpu.VMEM((tt, D), F32)],
        compiler_params=_params("arbitrary", "arbitrary"),
        name="peer_mix",
    )(u2, u2, peer_u, peer_u, peer_u, peer_v, peer_v, peer_v, e1r, phir, e2)


def _final_kernel(f_ref, x1_ref, g2_ref, lg_ref, lb_ref, y_ref, *, alpha):
    h = alpha * x1_ref[...] + (1.0 + g2_ref[...]) * f_ref[...]
    y_ref[...] = _layernorm(h, lg_ref[...], lb_ref[...])


def _final(f, row0, x1, mod, per_token, tiles_per_seq, lg, lb, alpha, tt=256):
    T = x1.shape[0]
    spec = _mod_spec_fn(per_token, tt, tiles_per_seq)
    off = row0 // tt
    row = pl.BlockSpec((tt, D), lambda i: (i, 0))
    const = pl.BlockSpec((1, D), lambda i: (0, 0))
    return pl.pallas_call(
        functools.partial(_final_kernel, alpha=alpha),
        grid=(T // tt,),
        in_specs=[pl.BlockSpec((tt, D), lambda i: (off + i, 0)), row, spec(5), const, const],
        out_specs=row,
        out_shape=jax.ShapeDtypeStruct((T, D), F32),
        compiler_params=_params("arbitrary"),
        name="final_ln",
    )(f, x1, mod, lg, lb)


def _pad_rows(w, n):
    return jnp.pad(w, ((0, n - w.shape[0]), (0, 0)))


def _layer(xp, xs, c_all, n_p, state_a, state_b, w_mod, b_mod, w_in, conv_a_w, conv_a_b, ln_a_g,
           ln_a_b, w_a_out, conv_b_w, w_b_out, w_o, ln1_g, ln1_b, w_q, sub_keys, peer_u, peer_v,
           ln2_g, ln2_b, alpha, seq_len, n_pos):
    n_s = xs.shape[0] // n_pos
    Tp = xp.shape[0]
    row = lambda v: v.reshape(1, D)

    mod = _modulation(c_all, w_mod, b_mod)
    mod_p = mod[:n_p].reshape(n_p, 6, 1, D)
    mod_s = jnp.tile(mod[n_p:n_p + n_s].reshape(n_s, 6, D).transpose(1, 0, 2), (1, n_pos, 1))

    w_in_b = w_in.astype(BF16)
    wa = _pad_rows(conv_a_w, TAPS_A_PAD)
    wb = _pad_rows(conv_b_w, SUBLANES)
    conv_args = (wa, row(conv_a_b), row(ln_a_g), row(ln_a_b), wb)

    glu_p, cgh_p, bg_p, sga_p, sgb_p = _in_proj(xp, mod_p, False, seq_len // 512, w_in_b)
    ya_p, yb_p = _conv_prompt(glu_p, cgh_p, bg_p, *conv_args, seq_len=seq_len)
    glu_s, cgh_s, bg_s, sga_s, sgb_s = _in_proj(xs, mod_s, True, 1, w_in_b)
    slab = lambda v: v.reshape(n_pos, n_s, D)
    ya_s, yb_s = _conv_sample(slab(glu_s), slab(cgh_s), slab(bg_s), state_a.transpose(1, 0, 2),
                              state_b.transpose(1, 0, 2), *conv_args)

    proj_w = (w_a_out, w_b_out, w_o.astype(BF16), row(ln1_g), row(ln1_b))
    x1_p, u2_p = _out_proj(ya_p, yb_p, sga_p, sgb_p, xp, mod_p, False, seq_len // 512, *proj_w, alpha)
    x1_s, u2_s = _out_proj(ya_s.reshape(-1, D), yb_s.reshape(-1, D), sga_s, sgb_s, xs, mod_s, True, 1,
                           *proj_w, alpha)

    u2 = jnp.concatenate([u2_p, u2_s], axis=0)
    keys = sub_keys.astype(BF16).reshape(2 * N_HEADS, N_KEYS, N_KEYS)
    st = _scores(u2, w_q.astype(BF16), keys)
    e1, e2, phi = _topk(st)
    f = _peer(u2, peer_u.astype(BF16), peer_v.astype(BF16), e1, phi, e2)

    y_p = _final(f, 0, x1_p, mod_p, False, seq_len // 256, row(ln2_g), row(ln2_b), alpha)
    y_s = _final(f, Tp, x1_s, mod_s, True, 1, row(ln2_g), row(ln2_b), alpha)

    new_a_p = glu_p.reshape(n_p, seq_len, D)[:, seq_len - (CONF_W - 1):]
    new_b_p = cgh_p.reshape(n_p, seq_len, D)[:, seq_len - (SC_W - 1):]
    glu_s_sm = slab(glu_s).transpose(1, 0, 2)
    cgh_s_sm = slab(cgh_s).transpose(1, 0, 2)
    new_a_s = jnp.concatenate([state_a, glu_s_sm], axis=1)[:, -(CONF_W - 1):]
    new_b_s = jnp.concatenate([state_b, cgh_s_sm], axis=1)[:, -(SC_W - 1):]
    return y_p, y_s, new_a_p, new_b_p, new_a_s, new_b_s


def kernel(x_prompt, x_sample, state_conv_a, state_conv_b, c_prompt, c_sample, w_mod, b_mod, w_in, conv_a_w, conv_a_b, ln_a_g, ln_a_b, w_a_out, conv_b_w, w_b_out, w_o, ln1_g, ln1_b, w_q, sub_keys, peer_u, peer_v, ln2_g, ln2_b):
    n_p, seq_len, d = x_prompt.shape
    n_s, n_pos, _ = x_sample.shape
    depth = w_mod.shape[0]
    assert d == D and seq_len % 512 == 0 and (n_s * n_pos) % 512 == 0 and n_s % 32 == 0
    alpha = (2.0 * depth) ** 0.25

    c_all = jnp.concatenate([c_prompt, c_sample], axis=0)
    c_all = _pad_rows(c_all, -(-c_all.shape[0] // SUBLANES) * SUBLANES)
    xp = x_prompt.reshape(n_p * seq_len, D)
    xs = x_sample.transpose(1, 0, 2).reshape(n_pos * n_s, D)

    layer_w = (w_mod, b_mod, w_in, conv_a_w, conv_a_b, ln_a_g, ln_a_b, w_a_out, conv_b_w, w_b_out,
               w_o, ln1_g, ln1_b, w_q, sub_keys, peer_u, peer_v, ln2_g, ln2_b)
    na_p, nb_p, na_s, nb_s = [], [], [], []
    for l in range(depth):
        xp, xs, a_p, b_p, a_s, b_s = _layer(
            xp, xs, c_all, n_p, state_conv_a[l], state_conv_b[l], *[w[l] for w in layer_w],
            alpha=alpha, seq_len=seq_len, n_pos=n_pos)
        na_p.append(a_p)
        nb_p.append(b_p)
        na_s.append(a_s)
        nb_s.append(b_s)
    y_prompt = xp.reshape(n_p, seq_len, D)
    y_sample = xs.reshape(n_pos, n_s, D).transpose(1, 0, 2)
    return (y_prompt, y_sample, jnp.stack(na_p), jnp.stack(nb_p), jnp.stack(na_s), jnp.stack(nb_s))
```

```python
import functools

import jax
import jax.numpy as jnp
from jax import lax
from jax.experimental import pallas as pl
from jax.experimental.pallas import tpu as pltpu

F32 = jnp.float32
BF16 = jnp.bfloat16

D = 2048
CONF_W = 31
SC_W = 3
N_HEADS = 8
N_KEYS = 128
TOPK = 16
LN_EPS = 1e-5
INV_SQRT2 = 0.7071067811865476
NEG_BIG = -3.0e38

LANES = 128
SUBLANES = 8
TAPS_A_PAD = -(-CONF_W // SUBLANES) * SUBLANES
VMEM_LIMIT = 56 * 1024 * 1024

def _params(*sem, flags=None):
    return pltpu.CompilerParams(dimension_semantics=sem, vmem_limit_bytes=VMEM_LIMIT, flags=flags)


def _layernorm(x, g, b):
    mu = jnp.mean(x, axis=-1, keepdims=True)
    xc = x - mu
    var = jnp.mean(xc * xc, axis=-1, keepdims=True)
    return xc * lax.rsqrt(var + LN_EPS) * g + b


def _mod_spec_fn(per_token, tt, tiles_per_seq):
    def spec(k):
        if per_token:
            return pl.BlockSpec((None, tt, D), lambda i, *_: (k, i, 0))
        return pl.BlockSpec((None, None, 1, D), lambda i, *_: (i // tiles_per_seq, k, 0, 0))
    return spec


def _mod_kernel(c_ref, w_ref, b_ref, o_ref):
    c = c_ref[...]
    s = (c * jax.nn.sigmoid(c)).astype(BF16)
    o_ref[...] = jnp.dot(s, w_ref[...].astype(BF16), preferred_element_type=F32) + b_ref[...]


def _modulation(c_all, w_mod, b_mod):
    n = c_all.shape[0]
    tn = 1024
    return pl.pallas_call(
        _mod_kernel,
        grid=(6 * D // tn,),
        in_specs=[pl.BlockSpec((n, D), lambda j: (0, 0)),
                  pl.BlockSpec((D, tn), lambda j: (0, j)),
                  pl.BlockSpec((1, tn), lambda j: (0, j))],
        out_specs=pl.BlockSpec((n, tn), lambda j: (0, j)),
        out_shape=jax.ShapeDtypeStruct((n, 6 * D), F32),
        compiler_params=_params("arbitrary"),
        name="modulation",
    )(c_all, w_mod, b_mod.reshape(1, 6 * D))


def _inproj_kernel(x_ref, sc_ref, sh_ref, w0, w1, w2, w3, w4, w5, w6,
                   glu_ref, cgh_ref, bg_ref, sga_ref, sgb_ref):
    u = (x_ref[...] * (1.0 + sc_ref[...]) + sh_ref[...]).astype(BF16)

    def z(w):
        return jnp.dot(u, w[...], preferred_element_type=F32)

    glu_ref[...] = z(w0) * jax.nn.sigmoid(z(w1))
    bg_ref[...] = z(w2)
    cgh_ref[...] = z(w3) * z(w4)
    sga_ref[...] = jax.nn.sigmoid(z(w5))
    sgb_ref[...] = jax.nn.sigmoid(z(w6))


def _in_proj(x2d, mod, per_token, tiles_per_seq, w_in, tt=512, tn=512):
    T = x2d.shape[0]
    spec = _mod_spec_fn(per_token, tt, tiles_per_seq)
    nj = D // tn
    w_specs = [pl.BlockSpec((D, tn), lambda i, j, g=g: (0, g * nj + j)) for g in range(7)]
    o_spec = pl.BlockSpec((tt, tn), lambda i, j: (i, j))
    return pl.pallas_call(
        _inproj_kernel,
        grid=(T // tt, nj),
        in_specs=[pl.BlockSpec((tt, D), lambda i, j: (i, 0)), spec(1), spec(0)] + w_specs,
        out_specs=[o_spec] * 5,
        out_shape=[jax.ShapeDtypeStruct((T, D), F32)] * 5,
        compiler_params=_params("arbitrary", "arbitrary"),
        name="in_proj",
    )(x2d, mod, mod, *([w_in] * 7))


CONV_CC = 256
CONV_RC = 128
HALO_A = 32
HALO_B = SUBLANES


def _conv_prompt_kernel(g_ref, gh_ref, c_ref, ch_ref, bg_ref, wa_ref, ba_ref, lg_ref, lb_ref,
                        wb_ref, ya_ref, yb_ref, ext_ref, extb_ref, pre_ref, sh_ref, *, tt, tiles_per_seq):
    i = pl.program_id(0)
    first = i % tiles_per_seq == 0

    @pl.when(first)
    def _():
        ext_ref[0:HALO_A, :] = jnp.zeros((HALO_A, D), F32)
        extb_ref[0:HALO_B, :] = jnp.zeros((HALO_B, D), F32)

    @pl.when(jnp.logical_not(first))
    def _():
        ext_ref[0:HALO_A, :] = gh_ref[...]
        extb_ref[0:HALO_B, :] = ch_ref[...]

    ext_ref[HALO_A:, :] = g_ref[...]
    extb_ref[HALO_B:, :] = c_ref[...]

    def chunk(c, carry):
        cs = pl.ds(pl.multiple_of(c * CONV_CC, CONV_CC), CONV_CC)
        wa = wa_ref[:, cs]
        wb = wb_ref[:, cs]
        bias = jnp.broadcast_to(ba_ref[:, cs], (CONV_RC, CONV_CC))
        for r in range(tt // CONV_RC):
            r0 = r * CONV_RC
            acc = bias
            lead = HALO_A - (CONF_W - 1)
            for b in range(SUBLANES):
                taps = [k for k in range(CONF_W) if (lead + k) % SUBLANES == b]
                span = (lead + taps[-1]) // SUBLANES * SUBLANES + CONV_RC
                if b:
                    sh_ref[b - 1, 0:span, :] = ext_ref[pl.ds(r0 + b, span), cs]
                for k in taps:
                    a8 = (lead + k) // SUBLANES * SUBLANES
                    win = (sh_ref[b - 1, a8:a8 + CONV_RC, :] if b
                           else ext_ref[pl.ds(r0 + a8, CONV_RC), cs])
                    acc = acc + win * wa[k:k + 1, :]
            pre_ref[pl.ds(r0, CONV_RC), cs] = acc
            accb = extb_ref[pl.ds(r0 + HALO_B - (SC_W - 1), CONV_RC), cs] * wb[0:1, :]
            for k in range(1, SC_W):
                accb = accb + extb_ref[pl.ds(r0 + HALO_B - (SC_W - 1) + k, CONV_RC), cs] * wb[k:k + 1, :]
            yb_ref[pl.ds(r0, CONV_RC), cs] = (bg_ref[pl.ds(r0, CONV_RC), cs] * accb).astype(BF16)
        return carry

    lax.fori_loop(0, D // CONV_CC, chunk, 0)
    y = _layernorm(pre_ref[...], lg_ref[...], lb_ref[...])
    ya_ref[...] = (y * jax.nn.sigmoid(y)).astype(BF16)


def _conv_prompt(glu, cgh, bg, wa, ba, lg, lb, wb, seq_len, tt=256):
    T = glu.shape[0]
    tiles_per_seq = seq_len // tt
    row = pl.BlockSpec((tt, D), lambda i: (i, 0))
    const = lambda shape: pl.BlockSpec(shape, lambda i: (0, 0))
    halo_a = pl.BlockSpec((HALO_A, D), lambda i: (jnp.maximum(i * (tt // HALO_A) - 1, 0), 0))
    halo_b = pl.BlockSpec((HALO_B, D), lambda i: (jnp.maximum(i * (tt // HALO_B) - 1, 0), 0))
    return pl.pallas_call(
        functools.partial(_conv_prompt_kernel, tt=tt, tiles_per_seq=tiles_per_seq),
        grid=(T // tt,),
        in_specs=[row, halo_a, row, halo_b, row, const((TAPS_A_PAD, D)), const((1, D)), const((1, D)),
                  const((1, D)), const((SUBLANES, D))],
        out_specs=[row, row],
        out_shape=[jax.ShapeDtypeStruct((T, D), BF16)] * 2,
        scratch_shapes=[pltpu.VMEM((tt + HALO_A, D), F32), pltpu.VMEM((tt + HALO_B, D), F32),
                        pltpu.VMEM((tt, D), F32), pltpu.VMEM((7, CONV_RC + HALO_A, CONV_CC), F32)],
        compiler_params=_params("arbitrary"),
        name="conv_prompt",
    )(glu, glu, cgh, cgh, bg, wa, ba, lg, lb, wb)


def _conv_sample_kernel(g_ref, sa_ref, c_ref, sb_ref, bg_ref, wa_ref, ba_ref, lg_ref, lb_ref,
                        wb_ref, ya_ref, yb_ref, pre_ref, *, n_pos, bs):
    n_hist_a = CONF_W - 1
    n_hist_b = SC_W - 1

    def chunk(c, carry):
        cs = pl.ds(pl.multiple_of(c * CONV_CC, CONV_CC), CONV_CC)
        wa = wa_ref[:, cs]
        wb = wb_ref[:, cs]
        bias = jnp.broadcast_to(ba_ref[:, cs], (bs, CONV_CC))
        acc = [bias] * n_pos
        for j in range(n_hist_a + n_pos):
            row = sa_ref[j, :, cs] if j < n_hist_a else g_ref[j - n_hist_a, :, cs]
            for p in range(n_pos):
                k = j - p
                if 0 <= k < CONF_W:
                    acc[p] = acc[p] + row * wa[k:k + 1, :]
        for p in range(n_pos):
            pre_ref[p, :, cs] = acc[p]
        rows_b = [sb_ref[j, :, cs] for j in range(n_hist_b)] + [c_ref[p, :, cs] for p in range(n_pos)]
        for p in range(n_pos):
            accb = rows_b[p] * wb[0:1, :]
            for k in range(1, SC_W):
                accb = accb + rows_b[p + k] * wb[k:k + 1, :]
            yb_ref[p, :, cs] = (bg_ref[p, :, cs] * accb).astype(BF16)
        return carry

    lax.fori_loop(0, D // CONV_CC, chunk, 0)
    for p in range(n_pos):
        y = _layernorm(pre_ref[p], lg_ref[...], lb_ref[...])
        ya_ref[p] = (y * jax.nn.sigmoid(y)).astype(BF16)


def _conv_sample(glu, cgh, bg, state_a, state_b, wa, ba, lg, lb, wb, bs=32):
    n_pos, n_seq, _ = glu.shape
    slab = pl.BlockSpec((n_pos, bs, D), lambda i: (0, i, 0))
    const = lambda shape: pl.BlockSpec(shape, lambda i: (0, 0))
    return pl.pallas_call(
        functools.partial(_conv_sample_kernel, n_pos=n_pos, bs=bs),
        grid=(n_seq // bs,),
        in_specs=[slab, pl.BlockSpec((CONF_W - 1, bs, D), lambda i: (0, i, 0)),
                  slab, pl.BlockSpec((SC_W - 1, bs, D), lambda i: (0, i, 0)),
                  slab, const((TAPS_A_PAD, D)), const((1, D)), const((1, D)), const((1, D)),
                  const((SUBLANES, D))],
        out_specs=[slab, slab],
        out_shape=[jax.ShapeDtypeStruct((n_pos, n_seq, D), BF16)] * 2,
        scratch_shapes=[pltpu.VMEM((n_pos, bs, D), F32)],
        compiler_params=_params("arbitrary"),
        name="conv_sample",
    )(glu, state_a, cgh, state_b, bg, wa, ba, lg, lb, wb)


def _branch_kernel(ya_ref, yb_ref, sga_ref, sgb_ref, wa_ref, wb_ref, mm_ref):
    ya = jnp.dot(ya_ref[...], wa_ref[...], preferred_element_type=F32)
    yb = jnp.dot(yb_ref[...], wb_ref[...], preferred_element_type=F32)
    mm_ref[...] = (sga_ref[...] * ya + sgb_ref[...] * yb).astype(BF16)


def _ln1_kernel(mm_ref, x_ref, g1_ref, sc2_ref, sh2_ref, wo_ref, lg_ref, lb_ref, x1_ref, u2_ref,
                *, alpha):
    m = jnp.dot(mm_ref[...], wo_ref[...], preferred_element_type=F32)
    h = alpha * x_ref[...] + (1.0 + g1_ref[...]) * m
    x1 = _layernorm(h, lg_ref[...], lb_ref[...])
    x1_ref[...] = x1
    u2_ref[...] = (x1 * (1.0 + sc2_ref[...]) + sh2_ref[...]).astype(BF16)


def _out_proj(ya_in, yb_in, sga, sgb, x2d, mod, per_token, tiles_per_seq, w_a, w_b, w_o,
              lg, lb, alpha, tt=512, tn=512):
    T = x2d.shape[0]
    tb = min(2 * tt, T)
    row2 = pl.BlockSpec((tb, D), lambda j, i: (i, 0))
    col2 = pl.BlockSpec((tb, tn), lambda j, i: (i, j))
    wcol = pl.BlockSpec((D, tn), lambda j, i: (0, j))
    mm = pl.pallas_call(
        _branch_kernel,
        grid=(D // tn, T // tb),
        in_specs=[row2, row2, col2, col2, wcol, wcol],
        out_specs=col2,
        out_shape=jax.ShapeDtypeStruct((T, D), BF16),
        compiler_params=_params("arbitrary", "arbitrary"),
        name="branch_proj",
    )(ya_in, yb_in, sga, sgb, w_a, w_b)

    spec = _mod_spec_fn(per_token, tt, tiles_per_seq)
    row = pl.BlockSpec((tt, D), lambda i: (i, 0))
    const = pl.BlockSpec((1, D), lambda i: (0, 0))
    return pl.pallas_call(
        functools.partial(_ln1_kernel, alpha=alpha),
        grid=(T // tt,),
        in_specs=[row, row, spec(2), spec(4), spec(3), pl.BlockSpec((D, D), lambda i: (0, 0)),
                  const, const],
        out_specs=[row, row],
        out_shape=[jax.ShapeDtypeStruct((T, D), F32), jax.ShapeDtypeStruct((T, D), BF16)],
        compiler_params=_params("arbitrary"),
        name="out_proj_ln1",
    )(mm, x2d, mod, mod, mod, w_o, lg, lb)


def _score_kernel(u2p_ref, u2s_ref, wq_ref, keys_ref, st_ref, u2_ref, *, n_p):
    i = pl.program_id(0)

    @pl.when(i < n_p)
    def _():
        u2_ref[...] = u2p_ref[...]

    @pl.when(i >= n_p)
    def _():
        u2_ref[...] = u2s_ref[...]

    q = jnp.dot(u2_ref[...], wq_ref[...], preferred_element_type=F32).astype(BF16)
    for c in range(2 * N_HEADS):
        st_ref[c * N_KEYS:(c + 1) * N_KEYS, :] = lax.dot_general(
            keys_ref[c], q[:, c * N_KEYS:(c + 1) * N_KEYS], (((1,), (1,)), ((), ())),
            preferred_element_type=F32)


def _scores(u2_p, u2_s, w_q, keys, tt=512):
    n_p = u2_p.shape[0] // tt
    T = u2_p.shape[0] + u2_s.shape[0]
    return pl.pallas_call(
        functools.partial(_score_kernel, n_p=n_p),
        grid=(T // tt,),
        in_specs=[pl.BlockSpec((tt, D), lambda i: (jnp.minimum(i, n_p - 1), 0)),
                  pl.BlockSpec((tt, D), lambda i: (jnp.maximum(i - n_p, 0), 0)),
                  pl.BlockSpec((D, D), lambda i: (0, 0)),
                  pl.BlockSpec((2 * N_HEADS, N_KEYS, N_KEYS), lambda i: (0, 0, 0))],
        out_specs=[pl.BlockSpec((D, tt), lambda i: (0, i)), pl.BlockSpec((tt, D), lambda i: (i, 0))],
        out_shape=[jax.ShapeDtypeStruct((D, T), F32), jax.ShapeDtypeStruct((T, D), BF16)],
        compiler_params=_params("arbitrary"),
        name="peer_scores",
    )(u2_p, u2_s, w_q, keys)


TOPK_SUB = 256


def _sort16_network():
    def merge(lo, hi, r):
        step = r * 2
        if step < hi - lo:
            yield from merge(lo, hi, step)
            yield from merge(lo + r, hi, step)
            yield from ((i, i + r) for i in range(lo + r, hi - r, step))
        else:
            yield (lo, lo + r)

    def sort(lo, hi):
        if hi > lo:
            mid = lo + (hi - lo) // 2
            yield from sort(lo, mid)
            yield from sort(mid + 1, hi)
            yield from merge(lo, hi, 1)

    return tuple(sort(0, TOPK - 1))


_SORT16 = _sort16_network()
_BITONIC16 = tuple((i, i + d) for d in (8, 4, 2, 1) for i in range(TOPK) if not i & d)


def _compare_exchange(v, net):
    for a, b in net:
        v[a], v[b] = jnp.maximum(v[a], v[b]), jnp.minimum(v[a], v[b])
    return v


def _top_values(s):
    assert s.shape[0] == SUBLANES * TOPK
    v = _compare_exchange([s[SUBLANES * m:SUBLANES * (m + 1), :] for m in range(TOPK)], _SORT16)
    for sh in (4, 2, 1):
        rot = [pltpu.roll(x, SUBLANES - sh, axis=0) for x in v]
        v = _compare_exchange([jnp.maximum(v[i], rot[TOPK - 1 - i]) for i in range(TOPK)], _BITONIC16)
    return [x[0:1, :] for x in v]


def _topk_kernel(st_ref, e1_ref, e2_ref, phi_ref, v1_ref, v2_ref, ev2_ref, aux_ref, *, tl):
    n_grp = tl // LANES
    big = -NEG_BIG
    c_any = TOPK // 2

    def head(h, carry):
        r1 = pl.multiple_of(h * 2 * N_KEYS, 2 * N_KEYS)
        r2 = r1 + N_KEYS
        for q in range(tl // TOPK_SUB):
            lo = q * TOPK_SUB
            for half, vref in ((r1, v1_ref), (r2, v2_ref)):
                vals = _top_values(st_ref[pl.ds(half, N_KEYS), lo:lo + TOPK_SUB])
                for r in range(TOPK):
                    for g in range(TOPK_SUB // LANES):
                        gg = q * (TOPK_SUB // LANES) + g
                        vref[r, gg:gg + 1, :] = vals[r][:, g * LANES:(g + 1) * LANES]
        v1 = [v1_ref[r] for r in range(TOPK)]
        v2 = [v2_ref[r] for r in range(TOPK)]
        top = [v1[0] + v2[c] for c in range(TOPK)]
        group = []
        for r in range(1, TOPK + 1):
            vals = [v1[r] + v2[c] for c in range(TOPK // (r + 1))] if r < TOPK else []
            if group and (r == TOPK or len(group) + len(vals) > TOPK):
                group += [jnp.full_like(top[0], NEG_BIG)] * (TOPK - len(group))
                group = _compare_exchange(group, _SORT16)
                top = _compare_exchange(
                    [jnp.maximum(top[i], group[TOPK - 1 - i]) for i in range(TOPK)], _BITONIC16)
                group = []
            group += vals
        tau = top[TOPK - 1]
        z = functools.reduce(lambda a, b: a + b, [jnp.exp(t - top[0]) for t in top])
        ev2 = [jnp.exp(v2[c] - v2[0]) for c in range(TOPK)]
        for c in range(TOPK):
            ev2_ref[c] = ev2[c]
        phi_top = functools.reduce(
            jnp.minimum, [jnp.where(v1[0] + v2[c] >= tau, ev2[c], big) for c in range(c_any, TOPK)])
        aux_ref[0] = tau
        aux_ref[1] = 1.0 / z
        aux_ref[2] = phi_top
        for g in range(n_grp):
            ls = slice(g * LANES, (g + 1) * LANES)
            row = lambda ref, r: ref[r, g:g + 1, :]
            s1 = st_ref[pl.ds(r1, N_KEYS), ls]
            s2 = st_ref[pl.ds(r2, N_KEYS), ls]
            tau_row = row(aux_ref, 0)
            phi = jnp.where(s1 + row(v2_ref, 0) >= tau_row, row(ev2_ref, 0), big)
            for c in range(1, c_any):
                phi = jnp.minimum(phi, jnp.where(s1 + row(v2_ref, c) >= tau_row, row(ev2_ref, c), big))
            phi = jnp.where(s1 >= row(v1_ref, 0), jnp.minimum(phi, row(aux_ref, 2)), phi)
            e1 = jnp.where(s1 >= row(v1_ref, TOPK - 1), jnp.exp(s1 - row(v1_ref, 0)), 0.0)
            e2 = jnp.where(s2 >= row(v2_ref, TOPK - 1), jnp.exp(s2 - row(v2_ref, 0)), 0.0)
            e1_ref[h, :, ls] = e1 * row(aux_ref, 1)
            e2_ref[h, :, ls] = e2
            phi_ref[h, :, ls] = phi
        return carry

    lax.fori_loop(0, N_HEADS, head, 0)


def _topk(st, tl=512):
    T = st.shape[1]
    n_grp = tl // LANES
    e_spec = pl.BlockSpec((N_HEADS, N_KEYS, tl), lambda i: (0, 0, i))
    return pl.pallas_call(
        functools.partial(_topk_kernel, tl=tl),
        grid=(T // tl,),
        in_specs=[pl.BlockSpec((D, tl), lambda i: (0, i))],
        out_specs=[e_spec, e_spec, e_spec],
        out_shape=[jax.ShapeDtypeStruct((N_HEADS, N_KEYS, T), F32)] * 3,
        scratch_shapes=[pltpu.VMEM((TOPK, n_grp, LANES), F32), pltpu.VMEM((TOPK, n_grp, LANES), F32),
                        pltpu.VMEM((TOPK, n_grp, LANES), F32), pltpu.VMEM((3, n_grp, LANES), F32)],
        compiler_params=_params("arbitrary"),
        name="peer_topk",
    )(st)


PEER_TC = 256


PEER_IROWS = SUBLANES


def _peer_gate_tile(row, col, act_ref, e1_ref, phi_ref, e2_ref, g_ref, *, tt):
    cols = slice(col * N_KEYS, (col + 1) * N_KEYS)
    for t0 in range(0, tt, PEER_TC):
        ts = slice(t0, t0 + PEER_TC)
        w = jnp.zeros((N_KEYS, PEER_TC), F32)
        for h in range(N_HEADS):
            e1row = e1_ref[h, row:row + 1, ts]
            phirow = phi_ref[h, row:row + 1, ts]
            e2 = e2_ref[h, :, ts]
            w = w + jnp.where(e2 >= phirow, e2, 0.0) * e1row
        g_ref[ts, cols] = (w.T * act_ref[ts, cols]).astype(BF16)


def _peer_kernel(u2_ref, u2n_ref, pu0_ref, pub_ref, pun_ref, pva_ref, pvb_ref, pvl_ref, e1_ref,
                 phi_ref, e2_ref, o_ref, acta_ref, actb_ref, ga_ref, gb_ref, ob_ref, *, tt, te, n_k):
    i = pl.program_id(0)
    k = pl.program_id(1)
    half = te // 2
    n_il = half // N_KEYS
    sub = half // 2
    wide = D // 2

    def act_of(tok_ref, rows):
        a = lax.dot_general(tok_ref[...], rows, (((1,), (1,)), ((), ())), preferred_element_type=F32)
        return 0.5 * a * (1.0 + lax.erf(a * INV_SQRT2))

    @pl.when(k == 0)
    def _():
        o_ref[...] = jnp.zeros((tt, D), F32)
        ob_ref[...] = jnp.zeros((tt, D), F32)
        gb_ref[...] = jnp.zeros((tt, half), BF16)

    @pl.when(jnp.logical_and(i == 0, k == 0))
    def _():
        acta_ref[...] = act_of(u2_ref, pu0_ref[...])

    def act_job(dst_ref, tok_ref, src_ref, j):
        def run():
            dst_ref[:, j * sub:(j + 1) * sub] = act_of(tok_ref, src_ref[j * sub:(j + 1) * sub, :])
        return run

    def mix_job(acc_ref, g_ref, v_ref, j):
        def run():
            cs = slice(j * wide, (j + 1) * wide)
            acc_ref[:, cs] += jnp.dot(g_ref[...], v_ref[:, cs], preferred_element_type=F32)
        return run

    jobs_a = [act_job(actb_ref, u2_ref, pub_ref, 0), mix_job(ob_ref, gb_ref, pvb_ref, 0),
              act_job(actb_ref, u2_ref, pub_ref, 1), mix_job(ob_ref, gb_ref, pvb_ref, 1)]
    jobs_b = [act_job(acta_ref, u2n_ref, pun_ref, 0), mix_job(o_ref, ga_ref, pva_ref, 0),
              act_job(acta_ref, u2n_ref, pun_ref, 1), mix_job(o_ref, ga_ref, pva_ref, 1)]
    gate = functools.partial(_peer_gate_tile, e1_ref=e1_ref, phi_ref=phi_ref, e2_ref=e2_ref, tt=tt)
    for il in range(n_il):
        jobs_a[il]()
        gate(il, il, acta_ref, g_ref=ga_ref)
    for il in range(n_il):
        jobs_b[il]()
        gate(n_il + il, il, actb_ref, g_ref=gb_ref)

    @pl.when(k == n_k - 1)
    def _():
        o_ref[...] += ob_ref[...] + jnp.dot(gb_ref[...], pvl_ref[...], preferred_element_type=F32)


def _peer(u2, peer_u, peer_v, e1, phi, e2, tt=512, te=1024):
    T = u2.shape[0]
    n_exp = peer_u.shape[0]
    assert te == PEER_IROWS * N_KEYS and n_exp % te == 0
    n_k = n_exp // te
    n_i = T // tt
    half = te // 2
    e1r = e1.reshape(N_HEADS, n_k, PEER_IROWS, T)
    phir = phi.reshape(N_HEADS, n_k, PEER_IROWS, T)
    rows_spec = pl.BlockSpec((N_HEADS, None, PEER_IROWS, tt), lambda i, k: (0, k, 0, i))
    halfs = lambda fn: pl.BlockSpec((half, D), lambda i, k: (fn(k), 0))
    last = 2 * n_k - 1
    return pl.pallas_call(
        functools.partial(_peer_kernel, tt=tt, te=te, n_k=n_k),
        grid=(T // tt, n_k),
        in_specs=[pl.BlockSpec((tt, D), lambda i, k: (i, 0)),
                  pl.BlockSpec((tt, D), lambda i, k: (jnp.minimum(i + (k + 1) // n_k, n_i - 1), 0)),
                  halfs(lambda k: 0),
                  halfs(lambda k: 2 * k + 1),
                  halfs(lambda k: (2 * k + 2) % (last + 1)),
                  halfs(lambda k: 2 * k),
                  halfs(lambda k: jnp.maximum(2 * k - 1, 0)),
                  halfs(lambda k: last),
                  rows_spec, rows_spec,
                  pl.BlockSpec((N_HEADS, N_KEYS, tt), lambda i, k: (0, 0, i))],
        out_specs=pl.BlockSpec((tt, D), lambda i, k: (i, 0)),
        out_shape=jax.ShapeDtypeStruct((T, D), F32),
        scratch_shapes=[pltpu.VMEM((tt, half), F32), pltpu.VMEM((tt, half), F32),
                        pltpu.VMEM((tt, half), BF16), pltpu.VMEM((tt, half), BF16),
                        pltpu.VMEM((tt, D), F32)],
        compiler_params=_params("arbitrary", "arbitrary"),
        name="peer_mix",
    )(u2, u2, peer_u, peer_u, peer_u, peer_v, peer_v, peer_v, e1r, phir, e2)


def _final_kernel(f_ref, x1_ref, g2_ref, lg_ref, lb_ref, y_ref, *, alpha):
    h = alpha * x1_ref[...] + (1.0 + g2_ref[...]) * f_ref[...]
    y_ref[...] = _layernorm(h, lg_ref[...], lb_ref[...])


def _final(f, row0, x1, mod, per_token, tiles_per_seq, lg, lb, alpha, tt=256):
    T = x1.shape[0]
    spec = _mod_spec_fn(per_token, tt, tiles_per_seq)
    off = row0 // tt
    row = pl.BlockSpec((tt, D), lambda i: (i, 0))
    const = pl.BlockSpec((1, D), lambda i: (0, 0))
    return pl.pallas_call(
        functools.partial(_final_kernel, alpha=alpha),
        grid=(T // tt,),
        in_specs=[pl.BlockSpec((tt, D), lambda i: (off + i, 0)), row, spec(5), const, const],
        out_specs=row,
        out_shape=jax.ShapeDtypeStruct((T, D), F32),
        compiler_params=_params("arbitrary"),
        name="final_ln",
    )(f, x1, mod, lg, lb)


def _pad_rows(w, n):
    return jnp.pad(w, ((0, n - w.shape[0]), (0, 0)))


def _layer(xp, xs, c_all, n_p, state_a, state_b, w_mod, b_mod, w_in, conv_a_w, conv_a_b, ln_a_g,
           ln_a_b, w_a_out, conv_b_w, w_b_out, w_o, ln1_g, ln1_b, w_q, sub_keys, peer_u, peer_v,
           ln2_g, ln2_b, alpha, seq_len, n_pos):
    n_s = xs.shape[0] // n_pos
    Tp = xp.shape[0]
    row = lambda v: v.reshape(1, D)

    mod = _modulation(c_all, w_mod, b_mod)
    mod_p = mod[:n_p].reshape(n_p, 6, 1, D)
    mod_s = jnp.tile(mod[n_p:n_p + n_s].reshape(n_s, 6, D).transpose(1, 0, 2), (1, n_pos, 1))

    w_in_b = w_in.astype(BF16)
    wa = _pad_rows(conv_a_w, TAPS_A_PAD)
    wb = _pad_rows(conv_b_w, SUBLANES)
    conv_args = (wa, row(conv_a_b), row(ln_a_g), row(ln_a_b), wb)

    glu_p, cgh_p, bg_p, sga_p, sgb_p = _in_proj(xp, mod_p, False, seq_len // 512, w_in_b)
    ya_p, yb_p = _conv_prompt(glu_p, cgh_p, bg_p, *conv_args, seq_len=seq_len)
    glu_s, cgh_s, bg_s, sga_s, sgb_s = _in_proj(xs, mod_s, True, 1, w_in_b)
    slab = lambda v: v.reshape(n_pos, n_s, D)
    ya_s, yb_s = _conv_sample(slab(glu_s), slab(cgh_s), slab(bg_s), state_a.transpose(1, 0, 2),
                              state_b.transpose(1, 0, 2), *conv_args)

    proj_w = (w_a_out.astype(BF16), w_b_out.astype(BF16), w_o.astype(BF16), row(ln1_g), row(ln1_b))
    x1_p, u2_p = _out_proj(ya_p, yb_p, sga_p, sgb_p, xp, mod_p, False, seq_len // 512, *proj_w, alpha)
    x1_s, u2_s = _out_proj(ya_s.reshape(-1, D), yb_s.reshape(-1, D), sga_s, sgb_s, xs, mod_s, True, 1,
                           *proj_w, alpha)

    keys = sub_keys.astype(BF16).reshape(2 * N_HEADS, N_KEYS, N_KEYS)
    st, u2 = _scores(u2_p, u2_s, w_q.astype(BF16), keys)
    e1, e2, phi = _topk(st)
    f = _peer(u2, peer_u.astype(BF16), peer_v.astype(BF16), e1, phi, e2)

    y_p = _final(f, 0, x1_p, mod_p, False, seq_len // 256, row(ln2_g), row(ln2_b), alpha)
    y_s = _final(f, Tp, x1_s, mod_s, True, 1, row(ln2_g), row(ln2_b), alpha)

    new_a_p = glu_p.reshape(n_p, seq_len, D)[:, seq_len - (CONF_W - 1):]
    new_b_p = cgh_p.reshape(n_p, seq_len, D)[:, seq_len - (SC_W - 1):]
    glu_s_sm = slab(glu_s).transpose(1, 0, 2)
    cgh_s_sm = slab(cgh_s).transpose(1, 0, 2)
    new_a_s = jnp.concatenate([state_a, glu_s_sm], axis=1)[:, -(CONF_W - 1):]
    new_b_s = jnp.concatenate([state_b, cgh_s_sm], axis=1)[:, -(SC_W - 1):]
    return y_p, y_s, new_a_p, new_b_p, new_a_s, new_b_s


def kernel(x_prompt, x_sample, state_conv_a, state_conv_b, c_prompt, c_sample, w_mod, b_mod, w_in, conv_a_w, conv_a_b, ln_a_g, ln_a_b, w_a_out, conv_b_w, w_b_out, w_o, ln1_g, ln1_b, w_q, sub_keys, peer_u, peer_v, ln2_g, ln2_b):
    n_p, seq_len, d = x_prompt.shape
    n_s, n_pos, _ = x_sample.shape
    depth = w_mod.shape[0]
    assert d == D and seq_len % 512 == 0 and (n_s * n_pos) % 512 == 0 and n_s % 32 == 0
    alpha = (2.0 * depth) ** 0.25

    c_all = jnp.concatenate([c_prompt, c_sample], axis=0)
    c_all = _pad_rows(c_all, -(-c_all.shape[0] // SUBLANES) * SUBLANES)
    xp = x_prompt.reshape(n_p * seq_len, D)
    xs = x_sample.transpose(1, 0, 2).reshape(n_pos * n_s, D)

    layer_w = (w_mod, b_mod, w_in, conv_a_w, conv_a_b, ln_a_g, ln_a_b, w_a_out, conv_b_w, w_b_out,
               w_o, ln1_g, ln1_b, w_q, sub_keys, peer_u, peer_v, ln2_g, ln2_b)
    na_p, nb_p, na_s, nb_s = [], [], [], []
    for l in range(depth):
        xp, xs, a_p, b_p, a_s, b_s = _layer(
            xp, xs, c_all, n_p, state_conv_a[l], state_conv_b[l], *[w[l] for w in layer_w],
            alpha=alpha, seq_len=seq_len, n_pos=n_pos)
        na_p.append(a_p)
        nb_p.append(b_p)
        na_s.append(a_s)
        nb_s.append(b_s)
    y_prompt = xp.reshape(n_p, seq_len, D)
    y_sample = xs.reshape(n_pos, n_s, D).transpose(1, 0, 2)
    return (y_prompt, y_sample, jnp.stack(na_p), jnp.stack(nb_p), jnp.stack(na_s), jnp.stack(nb_s))
```
